```python
import jax, jax.numpy as jnp
from jax import lax
import numpy as np

D_MODEL = 2048
BATCH = 4
SEQ = 4096
DEPTH = 4

N_MIXERS = 3
RMS_EPS = 1e-6
GLA_HEADS = 4
GLA_DK = D_MODEL // 2
GLA_DV = D_MODEL
GLA_HK = GLA_DK // GLA_HEADS
GLA_HV = GLA_DV // GLA_HEADS
GLA_LOWRANK = 16
GLA_TAU = 16.0
GLA_CHUNK = 64
SB_HEADS = 16
SB_HEAD_DIM = D_MODEL // SB_HEADS
SB_BLOCK = 128
RW_HEAD_DIM = 64
RW_HEADS = D_MODEL // RW_HEAD_DIM
RW_DECAY_LORA = max(32, int(round(D_MODEL ** 0.5 * 1.8 / 32)) * 32)
RW_AAA_LORA = max(32, int(round(D_MODEL ** 0.5 * 1.8 / 32)) * 32)
RW_GATE_LORA = max(32, int(round(D_MODEL ** 0.8 * 0.6 / 32)) * 32)
RW_GN_EPS = 64e-5
FFN_HIDDEN = -((-8 * D_MODEL) // (3 * 256)) * 256
N_GLA = (DEPTH + 2) // 3
N_SB = (DEPTH + 1) // 3
N_RW = DEPTH // 3

kernel_name = 'hybrid_gla_stickbreak_rwkv7'


def _f32(t):
    return t.astype(jnp.float32)


def rmsnorm(x, g):
    xf = _f32(x)
    y = xf * lax.rsqrt(jnp.mean(xf * xf, axis=-1, keepdims=True) + RMS_EPS)
    return (y * _f32(g)).astype(x.dtype)


def modulate(x, g, shift, scale):
    return rmsnorm(x, g) * (1.0 + scale[:, None, :]) + shift[:, None, :]


def swiglu(h, w_in, w_out):
    gate, up = jnp.split(h @ w_in, 2, axis=-1)
    return (jax.nn.silu(gate) * up) @ w_out


def gla_chunk_scan(q, k, v, log_alpha):
    B, T, H, HK = q.shape
    HV = v.shape[-1]
    n_chunks = T // GLA_CHUNK

    def to_chunks(t):
        return _f32(t).reshape(B, n_chunks, GLA_CHUNK, H, t.shape[-1]).transpose(1, 0, 3, 2, 4)

    causal = jnp.tril(jnp.ones((GLA_CHUNK, GLA_CHUNK), dtype=bool))[:, :, None]

    def step(S, inp):
        qc, kc, vc, lac = inp
        b = jnp.cumsum(lac, axis=2)
        inter = jnp.einsum('bhtk,bhkv->bhtv', qc * jnp.exp(b), S)
        rel = jnp.where(causal, b[:, :, :, None, :] - b[:, :, None, :, :], -jnp.inf)
        att = jnp.einsum('bhtk,bhsk,bhtsk->bhts', qc, kc, jnp.exp(rel))
        intra = jnp.einsum('bhts,bhsv->bhtv', att, vc)
        b_last = b[:, :, -1:, :]
        S = jnp.exp(b_last[:, :, 0, :])[..., None] * S + jnp.einsum('bhsk,bhsv->bhkv', kc * jnp.exp(b_last - b), vc)
        return S, inter + intra

    S0 = jnp.zeros((B, H, HK, HV), jnp.float32)
    _, o = lax.scan(step, S0, (to_chunks(q), to_chunks(k), to_chunks(v), to_chunks(log_alpha)))
    return o.transpose(1, 0, 3, 2, 4).reshape(B, T, H, HV)


def gla_mixer(h, w_in, w_alpha_up, b_alpha, norm_g, w_out):
    B, T, _ = h.shape
    splits = (GLA_DK, 2 * GLA_DK, 2 * GLA_DK + GLA_DV, 2 * GLA_DK + 2 * GLA_DV)
    q, k, v, g, a_low = jnp.split(h @ w_in, splits, axis=-1)
    log_alpha = jax.nn.log_sigmoid(_f32(a_low @ w_alpha_up + b_alpha)) / GLA_TAU
    q = q.reshape(B, T, GLA_HEADS, GLA_HK) * (GLA_HK ** -0.5)
    k = k.reshape(B, T, GLA_HEADS, GLA_HK)
    v = v.reshape(B, T, GLA_HEADS, GLA_HV)
    log_alpha = log_alpha.reshape(B, T, GLA_HEADS, GLA_HK)
    o = gla_chunk_scan(q, k, v, log_alpha).astype(h.dtype)
    o = rmsnorm(o, norm_g).reshape(B, T, GLA_DV)
    return (o * jax.nn.silu(g)) @ w_out


def stick_breaking_mixer(h, w_in, w_out):
    B, T, _ = h.shape
    q, k, v = jnp.split(h @ w_in, 3, axis=-1)

    def heads(t):
        return t.reshape(B, T, SB_HEADS, SB_HEAD_DIM).transpose(0, 2, 1, 3)

    q, k, v = heads(q), heads(k), heads(v)
    n_blocks = T // SB_BLOCK
    q_blocks = q.reshape(B, SB_HEADS, n_blocks, SB_BLOCK, SB_HEAD_DIM).transpose(2, 0, 1, 3, 4)
    key_pos = jnp.arange(T)

    def one_block(args):
        qb, blk = args
        z = _f32(jnp.einsum('bhtd,bhsd->bhts', qb, k)) * (SB_HEAD_DIM ** -0.5)
        q_pos = blk * SB_BLOCK + jnp.arange(SB_BLOCK)
        strict = key_pos[None, :] < q_pos[:, None]
        log_keep = jnp.where(strict, jax.nn.log_sigmoid(-z), 0.0)
        log_later = lax.cumsum(log_keep, axis=3, reverse=True) - log_keep
        w = jnp.where(strict, jnp.exp(jax.nn.log_sigmoid(z) + log_later), 0.0)
        return jnp.einsum('bhts,bhsd->bhtd', w.astype(v.dtype), v)

    o = lax.map(one_block, (q_blocks, jnp.arange(n_blocks)))
    o = o.transpose(1, 0, 3, 2, 4).reshape(B, T, D_MODEL)
    return o @ w_out


def rwkv7_scan(r, decay, k, v, a, b):
    B, T, H, N = r.shape

    def step(S, inp):
        r_t, w_t, k_t, v_t, a_t, b_t = inp
        sa = jnp.einsum('bhvk,bhk->bhv', S, a_t)
        S = S * w_t[:, :, None, :] + sa[..., None] * b_t[:, :, None, :] + v_t[..., None] * k_t[:, :, None, :]
        return S, jnp.einsum('bhvk,bhk->bhv', S, r_t)

    xs = tuple(jnp.moveaxis(_f32(t), 1, 0) for t in (r, decay, k, v, a, b))
    _, y = lax.scan(step, jnp.zeros((B, H, N, N), jnp.float32), xs)
    return jnp.moveaxis(y, 0, 1)


def rwkv7_mixer(h, mu, w_rkv, w0, w1, w2, a0, a1, a2, g1, g2, k_k, k_a, r_k, gn_g, gn_b, w_out):
    B, T, _ = h.shape
    xx = jnp.pad(h, ((0, 0), (1, 0), (0, 0)))[:, :-1] - h
    xr, xw, xk, xv, xa, xg = (h + xx * mu[n] for n in range(6))
    r = xr @ w_rkv[0]
    k = xk @ w_rkv[1]
    v = xv @ w_rkv[2]
    w_log = -jax.nn.softplus(-_f32(w0 + jnp.tanh(xw @ w1) @ w2)) - 0.5
    decay = jnp.exp(-jnp.exp(w_log))
    a = jax.nn.sigmoid(a0 + (xa @ a1) @ a2)
    g = jax.nn.sigmoid(xg @ g1) @ g2

    def heads(t):
        return t.reshape(B, T, RW_HEADS, RW_HEAD_DIM)

    kk = _f32(heads(k * k_k))
    kk = kk / jnp.maximum(jnp.sqrt(jnp.sum(kk * kk, axis=-1, keepdims=True)), 1e-12)
    k = k * (1.0 + (a - 1.0) * k_a)
    y = rwkv7_scan(heads(r), heads(decay), heads(k), heads(v), -kk, kk * _f32(heads(a)))
    mean = jnp.mean(y, axis=-1, keepdims=True)
    var = jnp.mean(jnp.square(y - mean), axis=-1, keepdims=True)
    y = ((y - mean) * lax.rsqrt(var + RW_GN_EPS)).reshape(B, T, D_MODEL) * _f32(gn_g) + _f32(gn_b)
    bonus = jnp.sum(_f32(heads(r * k * r_k)), axis=-1, keepdims=True) * _f32(heads(v))
    y = (y + bonus.reshape(B, T, D_MODEL)).astype(h.dtype) * g
    return y @ w_out


def setup_inputs(seed: int = 0) -> dict:
    key = jax.random.key(seed)
    ks = iter(jax.random.split(key, 40))
    D = D_MODEL

    def nrm(shape, scale):
        return scale * jax.random.normal(next(ks), shape, jnp.float32)

    return {
        'x': nrm((BATCH, SEQ, D), 1.0),
        'c': nrm((BATCH, D), 1.0),
        'norm_mix_g': 1.0 + nrm((DEPTH, D), 0.02),
        'norm_ffn_g': 1.0 + nrm((DEPTH, D), 0.02),
        'w_mod': nrm((DEPTH, D, 6 * D), 0.5 * D ** -0.5),
        'b_mod': nrm((DEPTH, 6 * D), 0.02),
        'ffn_w_in': nrm((DEPTH, D, 2 * FFN_HIDDEN), D ** -0.5),
        'ffn_w_out': nrm((DEPTH, FFN_HIDDEN, D), FFN_HIDDEN ** -0.5),
        'final_g': 1.0 + nrm((D,), 0.02),
        'gla_w_in': nrm((N_GLA, D, 2 * GLA_DK + 2 * GLA_DV + GLA_LOWRANK), D ** -0.5),
        'gla_w_alpha_up': nrm((N_GLA, GLA_LOWRANK, GLA_DK), GLA_LOWRANK ** -0.5),
        'gla_b_alpha': nrm((N_GLA, GLA_DK), 0.02),
        'gla_norm_g': 1.0 + nrm((N_GLA, GLA_HV), 0.02),
        'gla_w_out': nrm((N_GLA, GLA_DV, D), GLA_DV ** -0.5),
        'sb_w_in': nrm((N_SB, D, 3 * D), D ** -0.5),
        'sb_w_out': nrm((N_SB, D, D), D ** -0.5),
        'rw_mu': jax.random.uniform(next(ks), (N_RW, 6, D), jnp.float32),
        'rw_w_rkv': nrm((N_RW, 3, D, D), D ** -0.5),
        'rw_w0': nrm((N_RW, D), 0.5),
        'rw_w1': nrm((N_RW, D, RW_DECAY_LORA), D ** -0.5),
        'rw_w2': nrm((N_RW, RW_DECAY_LORA, D), 0.1 * RW_DECAY_LORA ** -0.5),
        'rw_a0': nrm((N_RW, D), 0.1),
        'rw_a1': nrm((N_RW, D, RW_AAA_LORA), D ** -0.5),
        'rw_a2': nrm((N_RW, RW_AAA_LORA, D), 0.1 * RW_AAA_LORA ** -0.5),
        'rw_g1': nrm((N_RW, D, RW_GATE_LORA), D ** -0.5),
        'rw_g2': nrm((N_RW, RW_GATE_LORA, D), RW_GATE_LORA ** -0.5),
        'rw_k_k': 0.85 + nrm((N_RW, D), 0.05),
        'rw_k_a': 1.0 + nrm((N_RW, D), 0.05),
        'rw_r_k': nrm((N_RW, D), 0.1),
        'rw_gn_g': 1.0 + nrm((N_RW, D), 0.02),
        'rw_gn_b': nrm((N_RW, D), 0.02),
        'rw_w_out': nrm((N_RW, D, D), D ** -0.5),
    }


def reference(x, c, norm_mix_g, norm_ffn_g, w_mod, b_mod, ffn_w_in, ffn_w_out, final_g,
              gla_w_in, gla_w_alpha_up, gla_b_alpha, gla_norm_g, gla_w_out,
              sb_w_in, sb_w_out,
              rw_mu, rw_w_rkv, rw_w0, rw_w1, rw_w2, rw_a0, rw_a1, rw_a2, rw_g1, rw_g2,
              rw_k_k, rw_k_a, rw_r_k, rw_gn_g, rw_gn_b, rw_w_out):
    c_act = jax.nn.silu(c)
    for i in range(DEPTH):
        mod = c_act @ w_mod[i] + b_mod[i]
        sh_m, sc_m, gt_m, sh_f, sc_f, gt_f = jnp.split(mod, 6, axis=-1)
        h = modulate(x, norm_mix_g[i], sh_m, sc_m)
        kind, j = i % N_MIXERS, i // N_MIXERS
        if kind == 0:
            y = gla_mixer(h, gla_w_in[j], gla_w_alpha_up[j], gla_b_alpha[j], gla_norm_g[j], gla_w_out[j])
        elif kind == 1:
            y = stick_breaking_mixer(h, sb_w_in[j], sb_w_out[j])
        else:
            y = rwkv7_mixer(h, rw_mu[j], rw_w_rkv[j], rw_w0[j], rw_w1[j], rw_w2[j],
                            rw_a0[j], rw_a1[j], rw_a2[j], rw_g1[j], rw_g2[j],
                            rw_k_k[j], rw_k_a[j], rw_r_k[j], rw_gn_g[j], rw_gn_b[j], rw_w_out[j])
        x = x + gt_m[:, None, :] * y
        h = modulate(x, norm_ffn_g[i], sh_f, sc_f)
        x = x + gt_f[:, None, :] * swiglu(h, ffn_w_in[i], ffn_w_out[i])
    return rmsnorm(x, final_g)
```

```python
import functools

import jax
import jax.numpy as jnp
from jax import lax
from jax.experimental import pallas as pl
from jax.experimental.pallas import tpu as pltpu

F32 = jnp.float32
MXU_DTYPE = jnp.bfloat16

V7X_LANES = 128
V7X_VMEM_LIMIT_BYTES = 56 * 1024 * 1024

RMS_EPS = 1e-6
GLA_HEADS = 4
GLA_LOWRANK = 16
GLA_TAU = 16.0
GLA_CHUNK = 64
SB_HEADS = 16
SB_TILE = 256
RW_HEAD_DIM = 64
RW_GN_EPS = 64e-5
RW_GROUP = 8
RW_TBLK = 128
RW_SUB = 32

_NT = (((1,), (1,)), ((), ()))
_TN = (((0,), (0,)), ((), ()))


def _params(semantics):
    return pltpu.CompilerParams(dimension_semantics=semantics,
                                vmem_limit_bytes=V7X_VMEM_LIMIT_BYTES)


def _row_tile(n_rows, seq, want):
    t = min(want, seq)
    assert seq % t == 0 and n_rows % t == 0
    return t


def _dot(a, b):
    return jnp.dot(a, b, preferred_element_type=F32)


def _split_dot(x, m):
    hi = x.astype(MXU_DTYPE)
    lo = (x - hi.astype(F32)).astype(MXU_DTYPE)
    return _dot(hi, m) + _dot(lo, m)


def _softplus(z):
    return jnp.maximum(z, 0.0) + jnp.log1p(jnp.exp(-jnp.abs(z)))


def _norm_mod(x, g, shift, scale):
    ms = jnp.mean(x * x, axis=-1, keepdims=True)
    y = x * lax.rsqrt(ms + RMS_EPS) * g
    return y * (1.0 + scale) + shift


def _mod_kernel(c_ref, w_ref, b_ref, o_ref):
    c = c_ref[...]
    ca = (c * jax.nn.sigmoid(c)).astype(MXU_DTYPE)
    o_ref[...] = _dot(ca, w_ref[...].astype(MXU_DTYPE)) + b_ref[...]


def _mod_all(c, w_mod, b_mod):
    depth, d, six_d = w_mod.shape
    bsz = c.shape[0]
    bp = -(-bsz // 8) * 8
    cp = jnp.pad(c, ((0, bp - bsz), (0, 0)))
    tn = 1024
    out = pl.pallas_call(
        _mod_kernel,
        out_shape=jax.ShapeDtypeStruct((depth, bp, six_d), F32),
        grid=(depth, six_d // tn),
        in_specs=[
            pl.BlockSpec((bp, d), lambda l, j: (0, 0)),
            pl.BlockSpec((None, d, tn), lambda l, j: (l, 0, j)),
            pl.BlockSpec((None, 1, tn), lambda l, j: (l, 0, j)),
        ],
        out_specs=pl.BlockSpec((None, bp, tn), lambda l, j: (l, 0, j)),
        compiler_params=_params(("arbitrary", "arbitrary")),
        name="mod_proj",
    )(cp, w_mod, b_mod.reshape(depth, 1, six_d))
    return out[:, :bsz]


def _nm_matmul_kernel(x_ref, g_ref, sh_ref, sc_ref, w_ref, o_ref, h_scr):
    @pl.when(pl.program_id(1) == 0)
    def _():
        h_scr[...] = _norm_mod(x_ref[...], g_ref[...], sh_ref[...], sc_ref[...]).astype(h_scr.dtype)

    o_ref[...] = _dot(h_scr[...], w_ref[...]).astype(o_ref.dtype)


def _nm_matmul(x2, seq, g, shift, scale, w, out_dtype, tn, name):
    n, d = x2.shape
    n_out = w.shape[1]
    tm = _row_tile(n, seq, 512)
    per_b = seq // tm
    bsz = n // seq
    return pl.pallas_call(
        _nm_matmul_kernel,
        out_shape=jax.ShapeDtypeStruct((n, n_out), out_dtype),
        grid=(n // tm, n_out // tn),
        in_specs=[
            pl.BlockSpec((tm, d), lambda i, j: (i, 0)),
            pl.BlockSpec((1, d), lambda i, j: (0, 0)),
            pl.BlockSpec((None, 1, d), lambda i, j: (i // per_b, 0, 0)),
            pl.BlockSpec((None, 1, d), lambda i, j: (i // per_b, 0, 0)),
            pl.BlockSpec((d, tn), lambda i, j: (0, j)),
        ],
        out_specs=pl.BlockSpec((tm, tn), lambda i, j: (i, j)),
        scratch_shapes=[pltpu.VMEM((tm, d), MXU_DTYPE)],
        compiler_params=_params(("parallel", "arbitrary")),
        name=name,
    )(x2, g.reshape(1, d), shift.reshape(bsz, 1, d), scale.reshape(bsz, 1, d), w)


def _res_matmul_kernel(a_ref, w_ref, x_ref, gt_ref, o_ref):
    o_ref[...] = x_ref[...] + gt_ref[...] * _dot(a_ref[...], w_ref[...])


def _res_matmul(a2, w, x2, seq, gate, name):
    n, k = a2.shape
    d = w.shape[1]
    tm = _row_tile(n, seq, 512)
    tn = 512
    per_b = seq // tm
    bsz = n // seq
    return pl.pallas_call(
        _res_matmul_kernel,
        out_shape=jax.ShapeDtypeStruct((n, d), F32),
        grid=(n // tm, d // tn),
        in_specs=[
            pl.BlockSpec((tm, k), lambda i, j: (i, 0)),
            pl.BlockSpec((k, tn), lambda i, j: (0, j)),
            pl.BlockSpec((tm, tn), lambda i, j: (i, j)),
            pl.BlockSpec((None, 1, tn), lambda i, j: (i // per_b, 0, j)),
        ],
        out_specs=pl.BlockSpec((tm, tn), lambda i, j: (i, j)),
        compiler_params=_params(("parallel", "arbitrary")),
        name=name,
    )(a2, w, x2, gate.reshape(bsz, 1, d))


def _ffn_kernel(x_ref, g_ref, sh_ref, sc_ref, gt_ref, wg_ref, wu_ref, wo_ref, o_ref, h_scr):
    k = pl.program_id(1)

    @pl.when(k == 0)
    def _():
        h_scr[...] = _norm_mod(x_ref[...], g_ref[...], sh_ref[...], sc_ref[...]).astype(h_scr.dtype)

    h = h_scr[...]
    gate = _dot(h, wg_ref[...])
    up = _dot(h, wu_ref[...])
    act = (gate * jax.nn.sigmoid(gate) * up).astype(MXU_DTYPE)
    part = _dot(act, wo_ref[...])

    @pl.when(k == 0)
    def _():
        o_ref[...] = part

    @pl.when(k > 0)
    def _():
        o_ref[...] += part

    @pl.when(k == pl.num_programs(1) - 1)
    def _():
        o_ref[...] = x_ref[...] + gt_ref[...] * o_ref[...]


def _ffn(x2, seq, g, shift, scale, gate, w_in, w_out):
    n, d = x2.shape
    hid = w_out.shape[0]
    tm = _row_tile(n, seq, 512)
    th = 512
    assert hid % th == 0
    nk = hid // th
    per_b = seq // tm
    bsz = n // seq
    bvec = lambda i, k: (i // per_b, 0, 0)
    return pl.pallas_call(
        _ffn_kernel,
        out_shape=jax.ShapeDtypeStruct((n, d), F32),
        grid=(n // tm, nk),
        in_specs=[
            pl.BlockSpec((tm, d), lambda i, k: (i, 0)),
            pl.BlockSpec((1, d), lambda i, k: (0, 0)),
            pl.BlockSpec((None, 1, d), bvec),
            pl.BlockSpec((None, 1, d), bvec),
            pl.BlockSpec((None, 1, d), bvec),
            pl.BlockSpec((d, th), lambda i, k: (0, k)),
            pl.BlockSpec((d, th), lambda i, k: (0, k + nk)),
            pl.BlockSpec((th, d), lambda i, k: (k, 0)),
        ],
        out_specs=pl.BlockSpec((tm, d), lambda i, k: (i, 0)),
        scratch_shapes=[pltpu.VMEM((tm, d), MXU_DTYPE)],
        compiler_params=_params(("parallel", "arbitrary")),
        name="ffn",
    )(x2, g.reshape(1, d), shift.reshape(bsz, 1, d), scale.reshape(bsz, 1, d),
      gate.reshape(bsz, 1, d), w_in, w_in, w_out)


def _rms_kernel(x_ref, g_ref, o_ref):
    x = x_ref[...]
    ms = jnp.mean(x * x, axis=-1, keepdims=True)
    o_ref[...] = x * lax.rsqrt(ms + RMS_EPS) * g_ref[...]


def _final_norm(x2, g):
    n, d = x2.shape
    tm = min(512, n)
    return pl.pallas_call(
        _rms_kernel,
        out_shape=jax.ShapeDtypeStruct((n, d), F32),
        grid=(n // tm,),
        in_specs=[pl.BlockSpec((tm, d), lambda i: (i, 0)),
                  pl.BlockSpec((1, d), lambda i: (0, 0))],
        out_specs=pl.BlockSpec((tm, d), lambda i: (i, 0)),
        compiler_params=_params(("parallel",)),
        name="final_norm",
    )(x2, g.reshape(1, d))


def _gla_kernel(q_ref, k_ref, v_ref, g_ref, al_ref, wup_ref, ba_ref, ng_ref, o_ref,
                st_scr, b_scr, k_scr, *, q_scale):
    c = GLA_CHUNK

    @pl.when(pl.program_id(2) == 0)
    def _():
        st_scr[...] = jnp.zeros_like(st_scr)

    rows = lax.broadcasted_iota(jnp.int32, (c, c), 0)
    cols = lax.broadcasted_iota(jnp.int32, (c, c), 1)
    tri = (rows >= cols).astype(MXU_DTYPE)
    causal = rows >= cols

    for ci in range(q_ref.shape[0] // c):
        sl = pl.ds(ci * c, c)
        q = q_ref[sl, :] * q_scale
        k = k_ref[sl, :]
        v = v_ref[sl, :].astype(MXU_DTYPE)
        pre = _dot(al_ref[sl, :].astype(MXU_DTYPE), wup_ref[...]) + ba_ref[...]
        la = -_softplus(-pre) * (1.0 / GLA_TAU)
        b = _split_dot_left(tri, la)
        b_last = b[c - 1:c, :]
        st = st_scr[...]
        inter = lax.dot_general((q * jnp.exp(b)).astype(MXU_DTYPE), st.astype(MXU_DTYPE), _NT,
                                preferred_element_type=F32)

        b_scr[...] = b
        k_scr[...] = k

        def col(j, att):
            bj = b_scr[pl.ds(j, 1), :]
            kj = k_scr[pl.ds(j, 1), :]
            e = jnp.exp(jnp.minimum(b - bj, 0.0))
            s = jnp.sum(q * kj * e, axis=1, keepdims=True)
            return jnp.where(cols == j, s, att)

        att = lax.fori_loop(0, c, col, jnp.zeros((c, c), F32))
        att = jnp.where(causal, att, 0.0)
        o = inter + _dot(att.astype(MXU_DTYPE), v)

        kd = (k * jnp.exp(b_last - b)).astype(MXU_DTYPE)
        st_scr[...] = jnp.exp(b_last) * st + lax.dot_general(v, kd, _TN, preferred_element_type=F32)

        ms = jnp.mean(o * o, axis=-1, keepdims=True)
        on = o * lax.rsqrt(ms + RMS_EPS) * ng_ref[...]
        gg = g_ref[sl, :]
        o_ref[sl, :] = (on * (gg * jax.nn.sigmoid(gg))).astype(o_ref.dtype)


def _split_dot_left(m, x):
    hi = x.astype(MXU_DTYPE)
    lo = (x - hi.astype(F32)).astype(MXU_DTYPE)
    return _dot(m, hi) + _dot(m, lo)


def _gla_scan(proj, alow, w_up, b_alpha, norm_g, bsz, seq):
    dk = w_up.shape[1]
    hk = dk // GLA_HEADS
    dv = proj.shape[-1] // 2 - dk
    hv = dv // GLA_HEADS
    tb = min(256, seq)
    proj3 = proj.reshape(bsz, seq, proj.shape[-1])
    alow3 = alow.reshape(bsz, seq, alow.shape[-1])
    kern = functools.partial(_gla_kernel, q_scale=hk ** -0.5)
    return pl.pallas_call(
        kern,
        out_shape=jax.ShapeDtypeStruct((bsz, seq, dv), MXU_DTYPE),
        grid=(bsz, GLA_HEADS, seq // tb),
        in_specs=[
            pl.BlockSpec((None, tb, hk), lambda b, h, t: (b, t, h)),
            pl.BlockSpec((None, tb, hk), lambda b, h, t: (b, t, GLA_HEADS + h)),
            pl.BlockSpec((None, tb, hv), lambda b, h, t: (b, t, (2 * dk) // hv + h)),
            pl.BlockSpec((None, tb, hv), lambda b, h, t: (b, t, (2 * dk + dv) // hv + h)),
            pl.BlockSpec((None, tb, alow3.shape[-1]), lambda b, h, t: (b, t, 0)),
            pl.BlockSpec((w_up.shape[0], hk), lambda b, h, t: (0, h)),
            pl.BlockSpec((1, hk), lambda b, h, t: (0, h)),
            pl.BlockSpec((1, hv), lambda b, h, t: (0, 0)),
        ],
        out_specs=pl.BlockSpec((None, tb, hv), lambda b, h, t: (b, t, h)),
        scratch_shapes=[pltpu.VMEM((hv, hk), F32), pltpu.VMEM((GLA_CHUNK, hk), F32),
                        pltpu.VMEM((GLA_CHUNK, hk), F32)],
        compiler_params=_params(("parallel", "parallel", "arbitrary")),
        name="gla_scan",
    )(proj3, proj3, proj3, proj3, alow3, w_up, b_alpha.reshape(1, dk), norm_g.reshape(1, hv))


def _gla_layer(x2, bsz, seq, norm_g_mix, shift, scale, gate, w_in, w_alpha_up, b_alpha, norm_g, w_out):
    d = x2.shape[1]
    dk = w_alpha_up.shape[1]
    main = 2 * dk + 2 * d
    w_main = w_in[:, :main].astype(MXU_DTYPE)
    w_low = jnp.pad(w_in[:, main:], ((0, 0), (0, V7X_LANES - GLA_LOWRANK))).astype(MXU_DTYPE)
    w_up = jnp.pad(w_alpha_up, ((0, V7X_LANES - GLA_LOWRANK), (0, 0))).astype(MXU_DTYPE)
    proj = _nm_matmul(x2, seq, norm_g_mix, shift, scale, w_main, F32, 512, "gla_proj")
    alow = _nm_matmul(x2, seq, norm_g_mix, shift, scale, w_low, F32, V7X_LANES, "gla_lowrank")
    og = _gla_scan(proj, alow, w_up, b_alpha, norm_g, bsz, seq)
    return _res_matmul(og.reshape(bsz * seq, d), w_out.astype(MXU_DTYPE), x2, seq, gate, "gla_out")


def _sb_kernel(q_ref, k_ref, v_ref, o_ref, acc_scr, run_scr, *, scale):
    tq = SB_TILE
    qi = pl.program_id(2)
    q = q_ref[...]
    rows = lax.broadcasted_iota(jnp.int32, (tq, tq), 0)
    cols = lax.broadcasted_iota(jnp.int32, (tq, tq), 1)
    strict = cols < rows
    later_m = jnp.concatenate([(rows > cols).astype(MXU_DTYPE), jnp.ones((tq, tq), MXU_DTYPE)], axis=1)

    def tile(kj, diagonal):
        start = pl.multiple_of(kj * tq, tq)
        ks = k_ref[pl.ds(start, tq), :]
        vs = v_ref[pl.ds(start, tq), :]
        z = lax.dot_general(q, ks, _NT, preferred_element_type=F32) * scale
        sp = _softplus(z)
        log_keep = -sp
        if diagonal:
            log_keep = jnp.where(strict, log_keep, 0.0)
        sums = _split_dot(log_keep, later_m)
        log_w = z - sp + sums[:, :tq] + run_scr[...]
        w = jnp.exp(log_w)
        if diagonal:
            w = jnp.where(strict, w, 0.0)
        acc_scr[...] += _dot(w.astype(MXU_DTYPE), vs)
        run_scr[...] += sums[:, tq:]

    acc_scr[...] = jnp.zeros_like(acc_scr)
    run_scr[...] = jnp.zeros_like(run_scr)
    tile(qi, True)

    def body(m, carry):
        tile(qi - 1 - m, False)
        return carry

    lax.fori_loop(0, qi, body, 0)
    o_ref[...] = acc_scr[...].astype(o_ref.dtype)


def _sb_attention(qkv, bsz, seq):
    d = qkv.shape[-1] // 3
    dh = d // SB_HEADS
    assert seq % SB_TILE == 0
    qkv3 = qkv.reshape(bsz, seq, 3 * d)
    kern = functools.partial(_sb_kernel, scale=dh ** -0.5)
    return pl.pallas_call(
        kern,
        out_shape=jax.ShapeDtypeStruct((bsz, seq, d), MXU_DTYPE),
        grid=(bsz, SB_HEADS, seq // SB_TILE),
        in_specs=[
            pl.BlockSpec((None, SB_TILE, dh), lambda b, h, i: (b, i, h)),
            pl.BlockSpec((None, seq, dh), lambda b, h, i: (b, 0, SB_HEADS + h)),
            pl.BlockSpec((None, seq, dh), lambda b, h, i: (b, 0, 2 * SB_HEADS + h)),
        ],
        out_specs=pl.BlockSpec((None, SB_TILE, dh), lambda b, h, i: (b, i, h)),
        scratch_shapes=[pltpu.VMEM((SB_TILE, dh), F32), pltpu.VMEM((SB_TILE, SB_TILE), F32)],
        compiler_params=_params(("parallel", "parallel", "arbitrary")),
        name="sb_attention",
    )(qkv3, qkv3, qkv3)


def _sb_layer(x2, bsz, seq, norm_g_mix, shift, scale, gate, w_in, w_out):
    d = x2.shape[1]
    qkv = _nm_matmul(x2, seq, norm_g_mix, shift, scale, w_in.astype(MXU_DTYPE), MXU_DTYPE, 512, "sb_proj")
    o = _sb_attention(qkv, bsz, seq)
    return _res_matmul(o.reshape(bsz * seq, d), w_out.astype(MXU_DTYPE), x2, seq, gate, "sb_out")


def _rw_proj_kernel(mix_ids_ref, x_ref, xp_ref, g_ref, sh_ref, sc_ref, mu_ref, w_ref, o_ref, mix_scr,
                    *, tiles_per_seq):
    i = pl.program_id(0)
    j = pl.program_id(1)

    @pl.when(j == 0)
    def _():
        g = g_ref[...]
        sh = sh_ref[...]
        sc = sc_ref[...]
        h = _norm_mod(x_ref[...], g, sh, sc)
        hp = _norm_mod(xp_ref[...], g, sh, sc)[7:8, :]
        hp = jnp.where(i % tiles_per_seq == 0, 0.0, hp)
        rows = lax.broadcasted_iota(jnp.int32, h.shape, 0)
        prev = jnp.where(rows == 0, hp, pltpu.roll(h, 1, 0))
        xx = prev - h
        for n in range(mix_scr.shape[0]):
            mix_scr[n] = (h + xx * mu_ref[n:n + 1, :]).astype(mix_scr.dtype)

    o_ref[...] = _dot(mix_scr[mix_ids_ref[j]], w_ref[...])


def _rw_proj(x2, seq, g, shift, scale, mu, w_all, mix_ids, tn):
    n, d = x2.shape
    n_out = w_all.shape[1]
    tm = _row_tile(n, seq, 512)
    per_b = seq // tm
    bsz = n // seq
    kern = functools.partial(_rw_proj_kernel, tiles_per_seq=per_b)
    bvec = lambda i, j, ids: (i // per_b, 0, 0)
    return pl.pallas_call(
        kern,
        out_shape=jax.ShapeDtypeStruct((n, n_out), F32),
        grid_spec=pltpu.PrefetchScalarGridSpec(
            num_scalar_prefetch=1,
            grid=(n // tm, n_out // tn),
            in_specs=[
                pl.BlockSpec((tm, d), lambda i, j, ids: (i, 0)),
                pl.BlockSpec((8, d), lambda i, j, ids: (jnp.maximum(i * (tm // 8) - 1, 0), 0)),
                pl.BlockSpec((1, d), lambda i, j, ids: (0, 0)),
                pl.BlockSpec((None, 1, d), bvec),
                pl.BlockSpec((None, 1, d), bvec),
                pl.BlockSpec((8, d), lambda i, j, ids: (0, 0)),
                pl.BlockSpec((d, tn), lambda i, j, ids: (0, j)),
            ],
            out_specs=pl.BlockSpec((tm, tn), lambda i, j, ids: (i, j)),
            scratch_shapes=[pltpu.VMEM((6, tm, d), MXU_DTYPE)],
        ),
        compiler_params=_params(("parallel", "arbitrary")),
        name="rwkv_proj",
    )(mix_ids, x2, x2, g.reshape(1, d), shift.reshape(bsz, 1, d), scale.reshape(bsz, 1, d),
      jnp.pad(mu, ((0, 2), (0, 0))), w_all)


def _lora_up_kernel(p_ref, w_ref, o_ref, *, act):
    p = p_ref[...]
    if act == "tanh":
        p = jnp.tanh(p)
    elif act == "sigmoid":
        p = jax.nn.sigmoid(p)
    o_ref[...] = _dot(p.astype(MXU_DTYPE), w_ref[...])


def _lora_up(proj, col_block, w2, act, tn_in, name):
    n = proj.shape[0]
    d = w2.shape[1]
    tm = min(512, n)
    kern = functools.partial(_lora_up_kernel, act=act)
    return pl.pallas_call(
        kern,
        out_shape=jax.ShapeDtypeStruct((n, d), F32),
        grid=(n // tm,),
        in_specs=[pl.BlockSpec((tm, tn_in), lambda i: (i, col_block)),
                  pl.BlockSpec((tn_in, d), lambda i: (0, 0))],
        out_specs=pl.BlockSpec((tm, d), lambda i: (i, 0)),
        compiler_params=_params(("parallel",)),
        name=name,
    )(proj, w2)


def _rw_scan_kernel(r_ref, k_ref, v_ref, wp_ref, ap_ref, gt_ref,
                    w0_ref, a0_ref, kk_ref, ka_ref, rk_ref, gng_ref, gnb_ref, o_ref,
                    s_scr, rows_scr, vbc_scr, yt_scr):
    hd = RW_HEAD_DIM
    tb, gl = r_ref.shape
    n_heads = gl // hd

    @pl.when(pl.program_id(2) == 0)
    def _():
        s_scr[...] = jnp.zeros_like(s_scr)

    yt_scr[...] = jnp.zeros_like(yt_scr)

    li = lax.broadcasted_iota(jnp.int32, (gl, gl), 0) // hd
    lj = lax.broadcasted_iota(jnp.int32, (gl, gl), 1) // hd
    head_ones = (li == lj).astype(MXU_DTYPE)

    r = r_ref[...]
    k = k_ref[...]
    v = v_ref[...]
    w_log = -_softplus(-(w0_ref[...] + wp_ref[...])) - 0.5
    decay = jnp.exp(-jnp.exp(w_log))
    a = jax.nn.sigmoid(a0_ref[...] + ap_ref[...])
    kk = k * kk_ref[...]
    kk = kk / jnp.maximum(jnp.sqrt(_split_dot(kk * kk, head_ones)), 1e-12)
    k2 = k * (1.0 + (a - 1.0) * ka_ref[...])

    ops = (-kk, decay, kk * a, k2, r)
    for n, val in enumerate(ops):
        for h in range(n_heads):
            rows_scr[n, h] = val[:, h * hd:(h + 1) * hd]
    v_t = [v[:, c * V7X_LANES:(c + 1) * V7X_LANES].T for c in range(gl // V7X_LANES)]
    lane_t = lax.broadcasted_iota(jnp.int32, (hd, tb), 1)

    for sb in range(tb // RW_SUB):
        for h in range(n_heads):
            vt_h = v_t[(h * hd) // V7X_LANES][(h * hd) % V7X_LANES:(h * hd) % V7X_LANES + hd, :]
            for tt in range(RW_SUB):
                t = sb * RW_SUB + tt
                vbc_scr[h, tt] = jnp.broadcast_to(vt_h[:, t:t + 1], (hd, hd))

        def step(tt, carry, sb=sb):
            t = sb * RW_SUB + tt
            for h in range(n_heads):
                a_row = rows_scr[0, h, pl.ds(t, 1), :]
                w_row = rows_scr[1, h, pl.ds(t, 1), :]
                b_row = rows_scr[2, h, pl.ds(t, 1), :]
                k_row = rows_scr[3, h, pl.ds(t, 1), :]
                r_row = rows_scr[4, h, pl.ds(t, 1), :]
                s = s_scr[h]
                sa = jnp.sum(s * a_row, axis=1, keepdims=True)
                s = s * w_row + sa * b_row + vbc_scr[h, tt] * k_row
                s_scr[h] = s
                y = jnp.sum(s * r_row, axis=1, keepdims=True)
                yt_scr[h] = jnp.where(lane_t == t, y, yt_scr[h])
            return carry

        lax.fori_loop(0, RW_SUB, step, 0)

    per_blk = V7X_LANES // hd
    y = jnp.concatenate(
        [jnp.concatenate([yt_scr[c * per_blk + p] for p in range(per_blk)], axis=0).T
         for c in range(gl // V7X_LANES)], axis=1)

    inv = 1.0 / hd
    mean = _split_dot(y, head_ones) * inv
    dlt = y - mean
    var = _split_dot(dlt * dlt, head_ones) * inv
    yn = dlt * lax.rsqrt(var + RW_GN_EPS) * gng_ref[...] + gnb_ref[...]
    bonus = _split_dot(r * k2 * rk_ref[...], head_ones) * v
    o_ref[...] = ((yn + bonus) * gt_ref[...]).astype(o_ref.dtype)


def _rw_scan(proj, w_pre, a_pre, g_out, w0, a0, k_k, k_a, r_k, gn_g, gn_b, bsz, seq, d):
    gl = RW_GROUP * RW_HEAD_DIM
    assert seq % RW_TBLK == 0 and d % gl == 0
    n_grp = d // gl
    p3 = proj.reshape(bsz, seq, proj.shape[-1])
    act = lambda off: pl.BlockSpec((None, RW_TBLK, gl), lambda b, g, t: (b, t, off + g))
    vec = pl.BlockSpec((1, gl), lambda b, g, t: (0, g))
    as3 = lambda z: z.reshape(bsz, seq, d)
    as_row = lambda z: z.reshape(1, d)
    return pl.pallas_call(
        _rw_scan_kernel,
        out_shape=jax.ShapeDtypeStruct((bsz, seq, d), MXU_DTYPE),
        grid=(bsz, n_grp, seq // RW_TBLK),
        in_specs=[act(0), act(n_grp), act(2 * n_grp), act(0), act(0), act(0)] + [vec] * 7,
        out_specs=pl.BlockSpec((None, RW_TBLK, gl), lambda b, g, t: (b, t, g)),
        scratch_shapes=[
            pltpu.VMEM((RW_GROUP, RW_HEAD_DIM, RW_HEAD_DIM), F32),
            pltpu.VMEM((5, RW_GROUP, RW_TBLK, RW_HEAD_DIM), F32),
            pltpu.VMEM((RW_GROUP, RW_SUB, RW_HEAD_DIM, RW_HEAD_DIM), F32),
            pltpu.VMEM((RW_GROUP, RW_HEAD_DIM, RW_TBLK), F32),
        ],
        compiler_params=_params(("parallel", "parallel", "arbitrary")),
        name="rwkv_scan",
    )(p3, p3, p3, as3(w_pre), as3(a_pre), as3(g_out),
      as_row(w0), as_row(a0), as_row(k_k), as_row(k_a), as_row(r_k), as_row(gn_g), as_row(gn_b))


def _rw_layer(x2, bsz, seq, norm_g_mix, shift, scale, gate, mu, w_rkv, w0, w1, w2, a0, a1, a2, g1, g2,
              k_k, k_a, r_k, gn_g, gn_b, w_out):
    d = x2.shape[1]
    tn = 512
    pad_c = lambda m: jnp.pad(m, ((0, 0), (0, tn - m.shape[1])))
    pad_r = lambda m: jnp.pad(m, ((0, tn - m.shape[0]), (0, 0))).astype(MXU_DTYPE)
    w_all = jnp.concatenate([w_rkv[0], w_rkv[1], w_rkv[2], pad_c(w1), pad_c(a1), pad_c(g1)],
                            axis=1).astype(MXU_DTYPE)
    per = d // tn
    mix_ids = jnp.array([0] * per + [2] * per + [3] * per + [1, 4, 5], jnp.int32)
    proj = _rw_proj(x2, seq, norm_g_mix, shift, scale, mu, w_all, mix_ids, tn)
    w_pre = _lora_up(proj, 3 * per, pad_r(w2), "tanh", tn, "rwkv_decay_lora")
    a_pre = _lora_up(proj, 3 * per + 1, pad_r(a2), "none", tn, "rwkv_a_lora")
    g_out = _lora_up(proj, 3 * per + 2, pad_r(g2), "sigmoid", tn, "rwkv_gate_lora")
    yg = _rw_scan(proj, w_pre, a_pre, g_out, w0, a0, k_k, k_a, r_k, gn_g, gn_b, bsz, seq, d)
    return _res_matmul(yg.reshape(bsz * seq, d), w_out.astype(MXU_DTYPE), x2, seq, gate, "rwkv_out")


def kernel(x, c, norm_mix_g, norm_ffn_g, w_mod, b_mod, ffn_w_in, ffn_w_out, final_g, gla_w_in, gla_w_alpha_up, gla_b_alpha, gla_norm_g, gla_w_out, sb_w_in, sb_w_out, rw_mu, rw_w_rkv, rw_w0, rw_w1, rw_w2, rw_a0, rw_a1, rw_a2, rw_g1, rw_g2, rw_k_k, rw_k_a, rw_r_k, rw_gn_g, rw_gn_b, rw_w_out):
    bsz, seq, d = x.shape
    depth = w_mod.shape[0]
    mod = _mod_all(c, w_mod, b_mod)
    x2 = x.reshape(bsz * seq, d)
    for i in range(depth):
        sh_m, sc_m, gt_m, sh_f, sc_f, gt_f = (mod[i, :, n * d:(n + 1) * d] for n in range(6))
        kind, j = i % 3, i // 3
        if kind == 0:
            x2 = _gla_layer(x2, bsz, seq, norm_mix_g[i], sh_m, sc_m, gt_m, gla_w_in[j],
                            gla_w_alpha_up[j], gla_b_alpha[j], gla_norm_g[j], gla_w_out[j])
        elif kind == 1:
            x2 = _sb_layer(x2, bsz, seq, norm_mix_g[i], sh_m, sc_m, gt_m, sb_w_in[j], sb_w_out[j])
        else:
            x2 = _rw_layer(x2, bsz, seq, norm_mix_g[i], sh_m, sc_m, gt_m, rw_mu[j], rw_w_rkv[j],
                           rw_w0[j], rw_w1[j], rw_w2[j], rw_a0[j], rw_a1[j], rw_a2[j], rw_g1[j],
                           rw_g2[j], rw_k_k[j], rw_k_a[j], rw_r_k[j], rw_gn_g[j], rw_gn_b[j],
                           rw_w_out[j])
        x2 = _ffn(x2, seq, norm_ffn_g[i], sh_f, sc_f, gt_f,
                  ffn_w_in[i].astype(MXU_DTYPE), ffn_w_out[i].astype(MXU_DTYPE))
    return _final_norm(x2, final_g).reshape(bsz, seq, d)
```

```python
import functools

import jax
import jax.numpy as jnp
from jax import lax
from jax.experimental import pallas as pl
from jax.experimental.pallas import tpu as pltpu

F32 = jnp.float32
MXU_DTYPE = jnp.bfloat16

V7X_LANES = 128
V7X_VMEM_LIMIT_BYTES = 56 * 1024 * 1024

RMS_EPS = 1e-6
GLA_HEADS = 4
GLA_LOWRANK = 16
GLA_TAU = 16.0
GLA_CHUNK = 64
GLA_SUB = 16
SB_HEADS = 16
SB_QTILE = 512
SB_KTILE = 256
SB_ROW_SPLIT = 2
RW_HEAD_DIM = 64
RW_GN_EPS = 64e-5
RW_GROUP = 16
RW_TBLK = 2 * RW_HEAD_DIM

_NT = (((1,), (1,)), ((), ()))
_TN = (((0,), (0,)), ((), ()))


def _params(semantics):
    return pltpu.CompilerParams(dimension_semantics=semantics,
                                vmem_limit_bytes=V7X_VMEM_LIMIT_BYTES)


def _row_tile(n_rows, seq, want):
    t = min(want, seq)
    assert seq % t == 0 and n_rows % t == 0
    return t


def _dot(a, b):
    return jnp.dot(a, b, preferred_element_type=F32)


def _split_dot(x, m):
    hi = x.astype(MXU_DTYPE)
    lo = (x - hi.astype(F32)).astype(MXU_DTYPE)
    return _dot(hi, m) + _dot(lo, m)


def _softplus(z):
    return jnp.maximum(z, 0.0) + jnp.log1p(jnp.exp(-jnp.abs(z)))


def _norm_mod(x, g, shift, scale):
    ms = jnp.mean(x * x, axis=-1, keepdims=True)
    y = x * lax.rsqrt(ms + RMS_EPS) * g
    return y * (1.0 + scale) + shift


def _mod_kernel(c_ref, w_ref, b_ref, o_ref):
    c = c_ref[...]
    ca = (c * jax.nn.sigmoid(c)).astype(MXU_DTYPE)
    o_ref[...] = _dot(ca, w_ref[...].astype(MXU_DTYPE)) + b_ref[...]


def _mod_all(c, w_mod, b_mod):
    depth, d, six_d = w_mod.shape
    bsz = c.shape[0]
    bp = -(-bsz // 8) * 8
    cp = jnp.pad(c, ((0, bp - bsz), (0, 0)))
    tn = 1024
    out = pl.pallas_call(
        _mod_kernel,
        out_shape=jax.ShapeDtypeStruct((depth, bp, six_d), F32),
        grid=(depth, six_d // tn),
        in_specs=[
            pl.BlockSpec((bp, d), lambda l, j: (0, 0)),
            pl.BlockSpec((None, d, tn), lambda l, j: (l, 0, j)),
            pl.BlockSpec((None, 1, tn), lambda l, j: (l, 0, j)),
        ],
        out_specs=pl.BlockSpec((None, bp, tn), lambda l, j: (l, 0, j)),
        compiler_params=_params(("arbitrary", "arbitrary")),
        name="mod_proj",
    )(cp, w_mod, b_mod.reshape(depth, 1, six_d))
    return out[:, :bsz]


def _nm_matmul_kernel(x_ref, g_ref, sh_ref, sc_ref, w_ref, o_ref, h_scr):
    @pl.when(pl.program_id(1) == 0)
    def _():
        h_scr[...] = _norm_mod(x_ref[...], g_ref[...], sh_ref[...], sc_ref[...]).astype(h_scr.dtype)

    o_ref[...] = _dot(h_scr[...], w_ref[...]).astype(o_ref.dtype)


def _nm_matmul(x2, seq, g, shift, scale, w, out_dtype, tn, name):
    n, d = x2.shape
    n_out = w.shape[1]
    tm = _row_tile(n, seq, 512)
    per_b = seq // tm
    bsz = n // seq
    return pl.pallas_call(
        _nm_matmul_kernel,
        out_shape=jax.ShapeDtypeStruct((n, n_out), out_dtype),
        grid=(n // tm, n_out // tn),
        in_specs=[
            pl.BlockSpec((tm, d), lambda i, j: (i, 0)),
            pl.BlockSpec((1, d), lambda i, j: (0, 0)),
            pl.BlockSpec((None, 1, d), lambda i, j: (i // per_b, 0, 0)),
            pl.BlockSpec((None, 1, d), lambda i, j: (i // per_b, 0, 0)),
            pl.BlockSpec((d, tn), lambda i, j: (0, j)),
        ],
        out_specs=pl.BlockSpec((tm, tn), lambda i, j: (i, j)),
        scratch_shapes=[pltpu.VMEM((tm, d), MXU_DTYPE)],
        compiler_params=_params(("parallel", "arbitrary")),
        name=name,
    )(x2, g.reshape(1, d), shift.reshape(bsz, 1, d), scale.reshape(bsz, 1, d), w)


def _res_matmul_kernel(a_ref, w_ref, x_ref, gt_ref, o_ref):
    o_ref[...] = x_ref[...] + gt_ref[...] * _dot(a_ref[...], w_ref[...])


def _res_matmul(a2, w, x2, seq, gate, name):
    n, k = a2.shape
    d = w.shape[1]
    tm = _row_tile(n, seq, 512)
    tn = 512
    per_b = seq // tm
    bsz = n // seq
    return pl.pallas_call(
        _res_matmul_kernel,
        out_shape=jax.ShapeDtypeStruct((n, d), F32),
        grid=(n // tm, d // tn),
        in_specs=[
            pl.BlockSpec((tm, k), lambda i, j: (i, 0)),
            pl.BlockSpec((k, tn), lambda i, j: (0, j)),
            pl.BlockSpec((tm, tn), lambda i, j: (i, j)),
            pl.BlockSpec((None, 1, tn), lambda i, j: (i // per_b, 0, j)),
        ],
        out_specs=pl.BlockSpec((tm, tn), lambda i, j: (i, j)),
        compiler_params=_params(("parallel", "arbitrary")),
        name=name,
    )(a2, w, x2, gate.reshape(bsz, 1, d))


def _ffn_kernel(x_ref, g_ref, sh_ref, sc_ref, gt_ref, wg_ref, wu_ref, wo_ref, o_ref, h_scr):
    k = pl.program_id(1)

    @pl.when(k == 0)
    def _():
        h_scr[...] = _norm_mod(x_ref[...], g_ref[...], sh_ref[...], sc_ref[...]).astype(h_scr.dtype)

    h = h_scr[...]
    gate = _dot(h, wg_ref[...])
    up = _dot(h, wu_ref[...])
    act = (gate * jax.nn.sigmoid(gate) * up).astype(MXU_DTYPE)
    part = _dot(act, wo_ref[...])

    @pl.when(k == 0)
    def _():
        o_ref[...] = part

    @pl.when(k > 0)
    def _():
        o_ref[...] += part

    @pl.when(k == pl.num_programs(1) - 1)
    def _():
        o_ref[...] = x_ref[...] + gt_ref[...] * o_ref[...]


def _ffn(x2, seq, g, shift, scale, gate, w_in, w_out):
    n, d = x2.shape
    hid = w_out.shape[0]
    tm = _row_tile(n, seq, 512)
    th = 512
    assert hid % th == 0
    nk = hid // th
    per_b = seq // tm
    bsz = n // seq
    bvec = lambda i, k: (i // per_b, 0, 0)
    return pl.pallas_call(
        _ffn_kernel,
        out_shape=jax.ShapeDtypeStruct((n, d), F32),
        grid=(n // tm, nk),
        in_specs=[
            pl.BlockSpec((tm, d), lambda i, k: (i, 0)),
            pl.BlockSpec((1, d), lambda i, k: (0, 0)),
            pl.BlockSpec((None, 1, d), bvec),
            pl.BlockSpec((None, 1, d), bvec),
            pl.BlockSpec((None, 1, d), bvec),
            pl.BlockSpec((d, th), lambda i, k: (0, k)),
            pl.BlockSpec((d, th), lambda i, k: (0, k + nk)),
            pl.BlockSpec((th, d), lambda i, k: (k, 0)),
        ],
        out_specs=pl.BlockSpec((tm, d), lambda i, k: (i, 0)),
        scratch_shapes=[pltpu.VMEM((tm, d), MXU_DTYPE)],
        compiler_params=_params(("parallel", "arbitrary")),
        name="ffn",
    )(x2, g.reshape(1, d), shift.reshape(bsz, 1, d), scale.reshape(bsz, 1, d),
      gate.reshape(bsz, 1, d), w_in, w_in, w_out)


def _rms_kernel(x_ref, g_ref, o_ref):
    x = x_ref[...]
    ms = jnp.mean(x * x, axis=-1, keepdims=True)
    o_ref[...] = x * lax.rsqrt(ms + RMS_EPS) * g_ref[...]


def _final_norm(x2, g):
    n, d = x2.shape
    tm = min(512, n)
    return pl.pallas_call(
        _rms_kernel,
        out_shape=jax.ShapeDtypeStruct((n, d), F32),
        grid=(n // tm,),
        in_specs=[pl.BlockSpec((tm, d), lambda i: (i, 0)),
                  pl.BlockSpec((1, d), lambda i: (0, 0))],
        out_specs=pl.BlockSpec((tm, d), lambda i: (i, 0)),
        compiler_params=_params(("parallel",)),
        name="final_norm",
    )(x2, g.reshape(1, d))


def _gla_kernel(q_ref, k_ref, v_ref, g_ref, al_ref, wup_ref, ba_ref, ng_ref, o_ref,
                st_scr, *, q_scale):
    c = GLA_CHUNK

    @pl.when(pl.program_id(2) == 0)
    def _():
        st_scr[...] = jnp.zeros_like(st_scr)

    rows = lax.broadcasted_iota(jnp.int32, (c, V7X_LANES), 0)
    cols = lax.broadcasted_iota(jnp.int32, (c, V7X_LANES), 1)
    causal = rows >= cols
    cols_sub = lax.broadcasted_iota(jnp.int32, (GLA_SUB, V7X_LANES), 1)
    tri = (lax.broadcasted_iota(jnp.int32, (c, c), 0)
           >= lax.broadcasted_iota(jnp.int32, (c, c), 1)).astype(MXU_DTYPE)

    for ci in range(q_ref.shape[0] // c):
        sl = pl.ds(ci * c, c)
        q = q_ref[sl, :] * q_scale
        k = k_ref[sl, :]
        v = v_ref[sl, :].astype(MXU_DTYPE)
        pre = _dot(al_ref[sl, :].astype(MXU_DTYPE), wup_ref[...]) + ba_ref[...]
        la = -_softplus(-pre) * (1.0 / GLA_TAU)
        b = _split_dot_left(tri, la)
        b_last = b[c - 1:c, :]
        st = st_scr[...]
        inter = lax.dot_general((q * jnp.exp(b)).astype(MXU_DTYPE), st.astype(MXU_DTYPE), _NT,
                                preferred_element_type=F32)

        sub = GLA_SUB
        att_rows = []
        for i in range(c // sub):
            lo_r, hi_r = i * sub, (i + 1) * sub
            q_i, b_i = q[lo_r:hi_r], b[lo_r:hi_r]
            if i == 0:
                att_i = jnp.zeros((sub, V7X_LANES), F32)
            else:
                ref = b[lo_r:lo_r + 1]
                q_f = (q_i * jnp.exp(b_i - ref)).astype(MXU_DTYPE)
                k_f = k[:lo_r] * jnp.exp(ref - b[:lo_r])
                k_f = jnp.concatenate([k_f, jnp.zeros((V7X_LANES - lo_r, k_f.shape[1]), F32)],
                                      axis=0).astype(MXU_DTYPE)
                att_i = lax.dot_general(q_f, k_f, _NT, preferred_element_type=F32)
            for j in range(sub):
                s_idx = lo_r + j
                e = jnp.exp(jnp.minimum(b_i - b[s_idx:s_idx + 1], 0.0))
                col = jnp.sum(q_i * k[s_idx:s_idx + 1] * e, axis=1, keepdims=True)
                att_i = jnp.where(cols_sub == s_idx, col, att_i)
            att_rows.append(att_i)
        att = jnp.where(causal, jnp.concatenate(att_rows, axis=0), 0.0)
        v_pad = jnp.concatenate([v_ref[sl, :], jnp.zeros((V7X_LANES - c, v.shape[1]), F32)],
                                axis=0).astype(MXU_DTYPE)
        o = inter + _dot(att.astype(MXU_DTYPE), v_pad)

        kd = (k * jnp.exp(b_last - b)).astype(MXU_DTYPE)
        st_scr[...] = jnp.exp(b_last) * st + lax.dot_general(v, kd, _TN, preferred_element_type=F32)

        ms = jnp.mean(o * o, axis=-1, keepdims=True)
        on = o * lax.rsqrt(ms + RMS_EPS) * ng_ref[...]
        gg = g_ref[sl, :]
        o_ref[sl, :] = (on * (gg * jax.nn.sigmoid(gg))).astype(o_ref.dtype)


def _split_dot_left(m, x):
    hi = x.astype(MXU_DTYPE)
    lo = (x - hi.astype(F32)).astype(MXU_DTYPE)
    return _dot(m, hi) + _dot(m, lo)


def _gla_scan(proj, alow, w_up, b_alpha, norm_g, bsz, seq):
    dk = w_up.shape[1]
    hk = dk // GLA_HEADS
    dv = proj.shape[-1] // 2 - dk
    hv = dv // GLA_HEADS
    tb = min(256, seq)
    proj3 = proj.reshape(bsz, seq, proj.shape[-1])
    alow3 = alow.reshape(bsz, seq, alow.shape[-1])
    kern = functools.partial(_gla_kernel, q_scale=hk ** -0.5)
    return pl.pallas_call(
        kern,
        out_shape=jax.ShapeDtypeStruct((bsz, seq, dv), MXU_DTYPE),
        grid=(bsz, GLA_HEADS, seq // tb),
        in_specs=[
            pl.BlockSpec((None, tb, hk), lambda b, h, t: (b, t, h)),
            pl.BlockSpec((None, tb, hk), lambda b, h, t: (b, t, GLA_HEADS + h)),
            pl.BlockSpec((None, tb, hv), lambda b, h, t: (b, t, (2 * dk) // hv + h)),
            pl.BlockSpec((None, tb, hv), lambda b, h, t: (b, t, (2 * dk + dv) // hv + h)),
            pl.BlockSpec((None, tb, alow3.shape[-1]), lambda b, h, t: (b, t, 0)),
            pl.BlockSpec((w_up.shape[0], hk), lambda b, h, t: (0, h)),
            pl.BlockSpec((1, hk), lambda b, h, t: (0, h)),
            pl.BlockSpec((1, hv), lambda b, h, t: (0, 0)),
        ],
        out_specs=pl.BlockSpec((None, tb, hv), lambda b, h, t: (b, t, h)),
        scratch_shapes=[pltpu.VMEM((hv, hk), F32)],
        compiler_params=_params(("parallel", "parallel", "arbitrary")),
        name="gla_scan",
    )(proj3, proj3, proj3, proj3, alow3, w_up, b_alpha.reshape(1, dk), norm_g.reshape(1, hv))


def _gla_layer(x2, bsz, seq, norm_g_mix, shift, scale, gate, w_in, w_alpha_up, b_alpha, norm_g, w_out):
    d = x2.shape[1]
    dk = w_alpha_up.shape[1]
    main = 2 * dk + 2 * d
    w_main = w_in[:, :main].astype(MXU_DTYPE)
    w_low = jnp.pad(w_in[:, main:], ((0, 0), (0, V7X_LANES - GLA_LOWRANK))).astype(MXU_DTYPE)
    w_up = jnp.pad(w_alpha_up, ((0, V7X_LANES - GLA_LOWRANK), (0, 0))).astype(MXU_DTYPE)
    proj = _nm_matmul(x2, seq, norm_g_mix, shift, scale, w_main, F32, 512, "gla_proj")
    alow = _nm_matmul(x2, seq, norm_g_mix, shift, scale, w_low, F32, V7X_LANES, "gla_lowrank")
    og = _gla_scan(proj, alow, w_up, b_alpha, norm_g, bsz, seq)
    return _res_matmul(og.reshape(bsz * seq, d), w_out.astype(MXU_DTYPE), x2, seq, gate, "gla_out")


def _sb_kernel(q_ref, k_ref, v_ref, o_ref, acc_scr, run_scr, *, scale):
    tq, tk = SB_QTILE, SB_KTILE
    hq = tq // SB_ROW_SPLIT
    qi = pl.program_id(2)
    kr = lax.broadcasted_iota(jnp.int32, (tk, tk), 0)
    kc = lax.broadcasted_iota(jnp.int32, (tk, tk), 1)
    later_m = (kr > kc).astype(MXU_DTYPE)
    later_m2 = jnp.concatenate([later_m, later_m], axis=0)
    q_pos = qi * tq + lax.broadcasted_iota(jnp.int32, (hq, tk), 0)
    k_off = lax.broadcasted_iota(jnp.int32, (hq, tk), 1)

    def tile(kj, masked):
        start = pl.multiple_of(kj * tk, tk)
        ks = k_ref[pl.ds(start, tk), :]
        vs = v_ref[pl.ds(start, tk), :]
        zs = [lax.dot_general(q_ref[h * hq:(h + 1) * hq, :], ks, _NT, preferred_element_type=F32) * scale
              for h in range(SB_ROW_SPLIT)]
        sps = []
        for h, z in enumerate(zs):
            sp = jnp.maximum(z, 0.0) + jnp.log(1.0 + jnp.exp(-jnp.abs(z)))
            if masked:
                sp = jnp.where(start + k_off < q_pos + h * hq, sp, 0.0)
            sps.append(sp)
        laters = []
        for sp in sps:
            hi = sp.astype(MXU_DTYPE)
            lo = (sp - hi.astype(F32)).astype(MXU_DTYPE)
            laters.append(_dot(jnp.concatenate([hi, lo], axis=1), later_m2))
        for h in range(SB_ROW_SPLIT):
            rs = slice(h * hq, (h + 1) * hq)
            run = run_scr[rs, :]
            w = jnp.exp(zs[h] - sps[h] - laters[h] - jnp.concatenate([run] * (tk // V7X_LANES), axis=1))
            if masked:
                w = jnp.where(start + k_off < q_pos + h * hq, w, 0.0)
            acc_scr[rs, :] += _dot(w.astype(MXU_DTYPE), vs)
            run_scr[rs, :] = run + laters[h][:, 0:1] + sps[h][:, 0:1]

    acc_scr[...] = jnp.zeros_like(acc_scr)
    run_scr[...] = jnp.zeros_like(run_scr)
    per_q = tq // tk
    for m in range(per_q):
        tile(qi * per_q + (per_q - 1 - m), True)

    def body(m, carry):
        tile(qi * per_q - 1 - m, False)
        return carry

    lax.fori_loop(0, qi * per_q, body, 0)
    o_ref[...] = acc_scr[...].astype(o_ref.dtype)


def _sb_attention(qkv, bsz, seq):
    d = qkv.shape[-1] // 3
    dh = d // SB_HEADS
    assert seq % SB_QTILE == 0 and SB_QTILE % SB_KTILE == 0
    qkv3 = qkv.reshape(bsz, seq, 3 * d)
    kern = functools.partial(_sb_kernel, scale=dh ** -0.5)
    return pl.pallas_call(
        kern,
        out_shape=jax.ShapeDtypeStruct((bsz, seq, d), MXU_DTYPE),
        grid=(bsz, SB_HEADS, seq // SB_QTILE),
        in_specs=[
            pl.BlockSpec((None, SB_QTILE, dh), lambda b, h, i: (b, i, h)),
            pl.BlockSpec((None, seq, dh), lambda b, h, i: (b, 0, SB_HEADS + h)),
            pl.BlockSpec((None, seq, dh), lambda b, h, i: (b, 0, 2 * SB_HEADS + h)),
        ],
        out_specs=pl.BlockSpec((None, SB_QTILE, dh), lambda b, h, i: (b, i, h)),
        scratch_shapes=[pltpu.VMEM((SB_QTILE, dh), F32), pltpu.VMEM((SB_QTILE, V7X_LANES), F32)],
        compiler_params=_params(("parallel", "parallel", "arbitrary")),
        name="sb_attention",
    )(qkv3, qkv3, qkv3)


def _sb_layer(x2, bsz, seq, norm_g_mix, shift, scale, gate, w_in, w_out):
    d = x2.shape[1]
    qkv = _nm_matmul(x2, seq, norm_g_mix, shift, scale, w_in.astype(MXU_DTYPE), MXU_DTYPE, 512, "sb_proj")
    o = _sb_attention(qkv, bsz, seq)
    return _res_matmul(o.reshape(bsz * seq, d), w_out.astype(MXU_DTYPE), x2, seq, gate, "sb_out")


def _rw_proj_kernel(mix_ids_ref, x_ref, xp_ref, g_ref, sh_ref, sc_ref, mu_ref, w_ref, o_ref, mix_scr,
                    *, tiles_per_seq):
    i = pl.program_id(0)
    j = pl.program_id(1)

    @pl.when(j == 0)
    def _():
        g = g_ref[...]
        sh = sh_ref[...]
        sc = sc_ref[...]
        h = _norm_mod(x_ref[...], g, sh, sc)
        hp = _norm_mod(xp_ref[...], g, sh, sc)[7:8, :]
        hp = jnp.where(i % tiles_per_seq == 0, 0.0, hp)
        rows = lax.broadcasted_iota(jnp.int32, h.shape, 0)
        prev = jnp.where(rows == 0, hp, pltpu.roll(h, 1, 0))
        xx = prev - h
        for n in range(mix_scr.shape[0]):
            mix_scr[n] = (h + xx * mu_ref[n:n + 1, :]).astype(mix_scr.dtype)

    o_ref[...] = _dot(mix_scr[mix_ids_ref[j]], w_ref[...])


def _rw_proj(x2, seq, g, shift, scale, mu, w_all, mix_ids, tn):
    n, d = x2.shape
    n_out = w_all.shape[1]
    tm = _row_tile(n, seq, 512)
    per_b = seq // tm
    bsz = n // seq
    kern = functools.partial(_rw_proj_kernel, tiles_per_seq=per_b)
    bvec = lambda i, j, ids: (i // per_b, 0, 0)
    return pl.pallas_call(
        kern,
        out_shape=jax.ShapeDtypeStruct((n, n_out), F32),
        grid_spec=pltpu.PrefetchScalarGridSpec(
            num_scalar_prefetch=1,
            grid=(n // tm, n_out // tn),
            in_specs=[
                pl.BlockSpec((tm, d), lambda i, j, ids: (i, 0)),
                pl.BlockSpec((8, d), lambda i, j, ids: (jnp.maximum(i * (tm // 8) - 1, 0), 0)),
                pl.BlockSpec((1, d), lambda i, j, ids: (0, 0)),
                pl.BlockSpec((None, 1, d), bvec),
                pl.BlockSpec((None, 1, d), bvec),
                pl.BlockSpec((8, d), lambda i, j, ids: (0, 0)),
                pl.BlockSpec((d, tn), lambda i, j, ids: (0, j)),
            ],
            out_specs=pl.BlockSpec((tm, tn), lambda i, j, ids: (i, j)),
            scratch_shapes=[pltpu.VMEM((6, tm, d), MXU_DTYPE)],
        ),
        compiler_params=_params(("parallel", "arbitrary")),
        name="rwkv_proj",
    )(mix_ids, x2, x2, g.reshape(1, d), shift.reshape(bsz, 1, d), scale.reshape(bsz, 1, d),
      jnp.pad(mu, ((0, 2), (0, 0))), w_all)


def _lora_up_kernel(p_ref, w_ref, o_ref, *, act):
    p = p_ref[...]
    if act == "tanh":
        p = jnp.tanh(p)
    elif act == "sigmoid":
        p = jax.nn.sigmoid(p)
    o_ref[...] = _dot(p.astype(MXU_DTYPE), w_ref[...])


def _lora_up(proj, col_block, w2, act, tn_in, name):
    n = proj.shape[0]
    d = w2.shape[1]
    tm = min(512, n)
    kern = functools.partial(_lora_up_kernel, act=act)
    return pl.pallas_call(
        kern,
        out_shape=jax.ShapeDtypeStruct((n, d), F32),
        grid=(n // tm,),
        in_specs=[pl.BlockSpec((tm, tn_in), lambda i: (i, col_block)),
                  pl.BlockSpec((tn_in, d), lambda i: (0, 0))],
        out_specs=pl.BlockSpec((tm, d), lambda i: (i, 0)),
        compiler_params=_params(("parallel",)),
        name=name,
    )(proj, w2)


def _rw_scan_kernel(r_ref, k_ref, v_ref, wp_ref, ap_ref, gt_ref,
                    w0_ref, a0_ref, kk_ref, ka_ref, rk_ref, gng_ref, gnb_ref, o_ref,
                    s_scr, rows_scr, yt_scr):
    hd = RW_HEAD_DIM
    tb, gl = r_ref.shape
    n_pairs = gl // V7X_LANES
    half = tb // 2

    @pl.when(pl.program_id(2) == 0)
    def _():
        s_scr[...] = jnp.zeros_like(s_scr)

    yt_scr[...] = jnp.zeros_like(yt_scr)

    li = lax.broadcasted_iota(jnp.int32, (V7X_LANES, V7X_LANES), 0) // hd
    lj = lax.broadcasted_iota(jnp.int32, (V7X_LANES, V7X_LANES), 1) // hd
    pair_ones = (li == lj).astype(MXU_DTYPE)
    pair_ones2 = jnp.concatenate([pair_ones, pair_ones], axis=0)

    def head_sum(x):
        return jnp.concatenate(
            [_split_dot(x[:, c * V7X_LANES:(c + 1) * V7X_LANES], pair_ones) for c in range(n_pairs)], axis=1)

    r = r_ref[...]
    k = k_ref[...]
    v = v_ref[...]
    w_log = -_softplus(-(w0_ref[...] + wp_ref[...])) - 0.5
    decay = jnp.exp(-jnp.exp(w_log))
    a = jax.nn.sigmoid(a0_ref[...] + ap_ref[...])
    kk = k * kk_ref[...]
    kk = kk / jnp.maximum(jnp.sqrt(head_sum(kk * kk)), 1e-12)
    k2 = k * (1.0 + (a - 1.0) * ka_ref[...])
    v_hi = v.astype(MXU_DTYPE).astype(F32)

    for n, val in enumerate((-kk, decay, kk * a, k2, r, v_hi, v - v_hi)):
        for p in range(n_pairs):
            rows_scr[n, p] = val[:, p * V7X_LANES:(p + 1) * V7X_LANES]

    rows64 = lax.broadcasted_iota(jnp.int32, (hd, V7X_LANES), 0)
    lanes64 = lax.broadcasted_iota(jnp.int32, (hd, V7X_LANES), 1) % hd
    eye2 = (rows64 == lanes64).astype(F32)

    def split_cat(x):
        hi = x.astype(MXU_DTYPE)
        lo = (x - hi.astype(F32)).astype(MXU_DTYPE)
        return jnp.concatenate([hi, lo], axis=1)

    shift = half.bit_length() - 1

    def put_y(p, u_plus, y):
        slot = 3 * p + lax.shift_right_logical(u_plus, shift)
        yt_scr[slot] = jnp.where(lanes64 == lax.bitwise_and(u_plus, half - 1), y, yt_scr[slot])

    def step(t, carry):
        t_prev = jnp.maximum(t - 1, 0)
        reds = []
        for p in range(n_pairs):
            row = lambda n, at=t: rows_scr[n, p, pl.ds(at, 1), :]
            s = s_scr[p]
            v_diag = jnp.concatenate([(eye2 * row(5)).astype(MXU_DTYPE),
                                      (eye2 * row(6)).astype(MXU_DTYPE)], axis=1)
            lhs = jnp.concatenate([split_cat(s * row(0)), split_cat(s * row(4, t_prev)), v_diag], axis=0)
            reds.append(_dot(lhs, pair_ones2))
        for p in range(n_pairs):
            row = lambda n: rows_scr[n, p, pl.ds(t, 1), :]
            sa = reds[p][:hd]
            v_col = reds[p][2 * hd:]
            s_scr[p] = s_scr[p] * row(1) + sa * row(2) + v_col * row(3)
            put_y(p, t + (half - 1), reds[p][hd:2 * hd])
        return carry

    lax.fori_loop(0, tb, step, 0)
    for p in range(n_pairs):
        y_last = _dot(split_cat(s_scr[p] * rows_scr[4, p, tb - 1:tb, :]), pair_ones2)
        put_y(p, jnp.int32(tb - 1 + half), y_last)

    def pair_rows(p):
        m = jnp.concatenate([yt_scr[3 * p + 1], yt_scr[3 * p + 2]], axis=0).T
        top = jnp.concatenate([m[:hd, :hd], m[hd:, :hd]], axis=1)
        bot = jnp.concatenate([m[:hd, hd:], m[hd:, hd:]], axis=1)
        return jnp.concatenate([top, bot], axis=0)

    y = jnp.concatenate([pair_rows(p) for p in range(n_pairs)], axis=1)

    inv = 1.0 / hd
    mean = head_sum(y) * inv
    dlt = y - mean
    var = head_sum(dlt * dlt) * inv
    yn = dlt * lax.rsqrt(var + RW_GN_EPS) * gng_ref[...] + gnb_ref[...]
    bonus = head_sum(r * k2 * rk_ref[...]) * v
    o_ref[...] = ((yn + bonus) * gt_ref[...]).astype(o_ref.dtype)


def _rw_scan(proj, w_pre, a_pre, g_out, w0, a0, k_k, k_a, r_k, gn_g, gn_b, bsz, seq, d):
    gl = RW_GROUP * RW_HEAD_DIM
    assert seq % RW_TBLK == 0 and d % gl == 0 and gl % V7X_LANES == 0
    n_grp = d // gl
    n_pairs = gl // V7X_LANES
    p3 = proj.reshape(bsz, seq, proj.shape[-1])
    act = lambda off: pl.BlockSpec((None, RW_TBLK, gl), lambda b, g, t: (b, t, off + g))
    vec = pl.BlockSpec((1, gl), lambda b, g, t: (0, g))
    as3 = lambda z: z.reshape(bsz, seq, d)
    as_row = lambda z: z.reshape(1, d)
    return pl.pallas_call(
        _rw_scan_kernel,
        out_shape=jax.ShapeDtypeStruct((bsz, seq, d), MXU_DTYPE),
        grid=(bsz, n_grp, seq // RW_TBLK),
        in_specs=[act(0), act(n_grp), act(2 * n_grp), act(0), act(0), act(0)] + [vec] * 7,
        out_specs=pl.BlockSpec((None, RW_TBLK, gl), lambda b, g, t: (b, t, g)),
        scratch_shapes=[
            pltpu.VMEM((n_pairs, RW_HEAD_DIM, V7X_LANES), F32),
            pltpu.VMEM((7, n_pairs, RW_TBLK, V7X_LANES), F32),
            pltpu.VMEM((3 * n_pairs, RW_HEAD_DIM, V7X_LANES), F32),
        ],
        compiler_params=_params(("parallel", "parallel", "arbitrary")),
        name="rwkv_scan",
    )(p3, p3, p3, as3(w_pre), as3(a_pre), as3(g_out),
      as_row(w0), as_row(a0), as_row(k_k), as_row(k_a), as_row(r_k), as_row(gn_g), as_row(gn_b))


def _rw_layer(x2, bsz, seq, norm_g_mix, shift, scale, gate, mu, w_rkv, w0, w1, w2, a0, a1, a2, g1, g2,
              k_k, k_a, r_k, gn_g, gn_b, w_out):
    d = x2.shape[1]
    tn = 512
    pad_c = lambda m: jnp.pad(m, ((0, 0), (0, tn - m.shape[1])))
    pad_r = lambda m: jnp.pad(m, ((0, tn - m.shape[0]), (0, 0))).astype(MXU_DTYPE)
    w_all = jnp.concatenate([w_rkv[0], w_rkv[1], w_rkv[2], pad_c(w1), pad_c(a1), pad_c(g1)],
                            axis=1).astype(MXU_DTYPE)
    per = d // tn
    mix_ids = jnp.array([0] * per + [2] * per + [3] * per + [1, 4, 5], jnp.int32)
    proj = _rw_proj(x2, seq, norm_g_mix, shift, scale, mu, w_all, mix_ids, tn)
    w_pre = _lora_up(proj, 3 * per, pad_r(w2), "tanh", tn, "rwkv_decay_lora")
    a_pre = _lora_up(proj, 3 * per + 1, pad_r(a2), "none", tn, "rwkv_a_lora")
    g_out = _lora_up(proj, 3 * per + 2, pad_r(g2), "sigmoid", tn, "rwkv_gate_lora")
    yg = _rw_scan(proj, w_pre, a_pre, g_out, w0, a0, k_k, k_a, r_k, gn_g, gn_b, bsz, seq, d)
    return _res_matmul(yg.reshape(bsz * seq, d), w_out.astype(MXU_DTYPE), x2, seq, gate, "rwkv_out")


def kernel(x, c, norm_mix_g, norm_ffn_g, w_mod, b_mod, ffn_w_in, ffn_w_out, final_g, gla_w_in, gla_w_alpha_up, gla_b_alpha, gla_norm_g, gla_w_out, sb_w_in, sb_w_out, rw_mu, rw_w_rkv, rw_w0, rw_w1, rw_w2, rw_a0, rw_a1, rw_a2, rw_g1, rw_g2, rw_k_k, rw_k_a, rw_r_k, rw_gn_g, rw_gn_b, rw_w_out):
    bsz, seq, d = x.shape
    depth = w_mod.shape[0]
    mod = _mod_all(c, w_mod, b_mod)
    x2 = x.reshape(bsz * seq, d)
    for i in range(depth):
        sh_m, sc_m, gt_m, sh_f, sc_f, gt_f = (mod[i, :, n * d:(n + 1) * d] for n in range(6))
        kind, j = i % 3, i // 3
        if kind == 0:
            x2 = _gla_layer(x2, bsz, seq, norm_mix_g[i], sh_m, sc_m, gt_m, gla_w_in[j],
                            gla_w_alpha_up[j], gla_b_alpha[j], gla_norm_g[j], gla_w_out[j])
        elif kind == 1:
            x2 = _sb_layer(x2, bsz, seq, norm_mix_g[i], sh_m, sc_m, gt_m, sb_w_in[j], sb_w_out[j])
        else:
            x2 = _rw_layer(x2, bsz, seq, norm_mix_g[i], sh_m, sc_m, gt_m, rw_mu[j], rw_w_rkv[j],
                           rw_w0[j], rw_w1[j], rw_w2[j], rw_a0[j], rw_a1[j], rw_a2[j], rw_g1[j],
                           rw_g2[j], rw_k_k[j], rw_k_a[j], rw_r_k[j], rw_gn_g[j], rw_gn_b[j],
                           rw_w_out[j])
        x2 = _ffn(x2, seq, norm_ffn_g[i], sh_f, sc_f, gt_f,
                  ffn_w_in[i].astype(MXU_DTYPE), ffn_w_out[i].astype(MXU_DTYPE))
    return _final_norm(x2, final_g).reshape(bsz, seq, d)
```

```python
import functools

import jax
import jax.numpy as jnp
from jax import lax
from jax.experimental import pallas as pl
from jax.experimental.pallas import tpu as pltpu

F32 = jnp.float32
MXU_DTYPE = jnp.bfloat16

V7X_LANES = 128
V7X_VMEM_LIMIT_BYTES = 56 * 1024 * 1024

RMS_EPS = 1e-6
GLA_HEADS = 4
GLA_LOWRANK = 16
GLA_TAU = 16.0
GLA_CHUNK = 64
GLA_SUB = 16
SB_HEADS = 16
SB_QTILE = 512
SB_KTILE = 256
SB_ROW_SPLIT = 2
RW_HEAD_DIM = 64
RW_GN_EPS = 64e-5
RW_GROUP = 16
RW_TBLK = 2 * RW_HEAD_DIM
RW_CHUNK = 16
RW_QUAD = 4

_NT = (((1,), (1,)), ((), ()))
_TN = (((0,), (0,)), ((), ()))


def _params(semantics):
    return pltpu.CompilerParams(dimension_semantics=semantics,
                                vmem_limit_bytes=V7X_VMEM_LIMIT_BYTES)


def _row_tile(n_rows, seq, want):
    t = min(want, seq)
    assert seq % t == 0 and n_rows % t == 0
    return t


def _dot(a, b):
    return jnp.dot(a, b, preferred_element_type=F32)


def _split_dot(x, m):
    hi = x.astype(MXU_DTYPE)
    lo = (x - hi.astype(F32)).astype(MXU_DTYPE)
    return _dot(hi, m) + _dot(lo, m)


def _softplus(z):
    return jnp.maximum(z, 0.0) + jnp.log1p(jnp.exp(-jnp.abs(z)))


def _norm_mod(x, g, shift, scale):
    ms = jnp.mean(x * x, axis=-1, keepdims=True)
    y = x * lax.rsqrt(ms + RMS_EPS) * g
    return y * (1.0 + scale) + shift


def _mod_kernel(c_ref, w_ref, b_ref, o_ref):
    c = c_ref[...]
    ca = (c * jax.nn.sigmoid(c)).astype(MXU_DTYPE)
    o_ref[...] = _dot(ca, w_ref[...].astype(MXU_DTYPE)) + b_ref[...]


def _mod_all(c, w_mod, b_mod):
    depth, d, six_d = w_mod.shape
    bsz = c.shape[0]
    bp = -(-bsz // 8) * 8
    cp = jnp.pad(c, ((0, bp - bsz), (0, 0)))
    tn = 1024
    out = pl.pallas_call(
        _mod_kernel,
        out_shape=jax.ShapeDtypeStruct((depth, bp, six_d), F32),
        grid=(depth, six_d // tn),
        in_specs=[
            pl.BlockSpec((bp, d), lambda l, j: (0, 0)),
            pl.BlockSpec((None, d, tn), lambda l, j: (l, 0, j)),
            pl.BlockSpec((None, 1, tn), lambda l, j: (l, 0, j)),
        ],
        out_specs=pl.BlockSpec((None, bp, tn), lambda l, j: (l, 0, j)),
        compiler_params=_params(("arbitrary", "arbitrary")),
        name="mod_proj",
    )(cp, w_mod, b_mod.reshape(depth, 1, six_d))
    return out[:, :bsz]


def _nm_matmul_kernel(x_ref, g_ref, sh_ref, sc_ref, w_ref, o_ref, h_scr):
    @pl.when(pl.program_id(1) == 0)
    def _():
        h_scr[...] = _norm_mod(x_ref[...], g_ref[...], sh_ref[...], sc_ref[...]).astype(h_scr.dtype)

    o_ref[...] = _dot(h_scr[...], w_ref[...]).astype(o_ref.dtype)


def _nm_matmul(x2, seq, g, shift, scale, w, out_dtype, tn, name):
    n, d = x2.shape
    n_out = w.shape[1]
    tm = _row_tile(n, seq, 512)
    per_b = seq // tm
    bsz = n // seq
    return pl.pallas_call(
        _nm_matmul_kernel,
        out_shape=jax.ShapeDtypeStruct((n, n_out), out_dtype),
        grid=(n // tm, n_out // tn),
        in_specs=[
            pl.BlockSpec((tm, d), lambda i, j: (i, 0)),
            pl.BlockSpec((1, d), lambda i, j: (0, 0)),
            pl.BlockSpec((None, 1, d), lambda i, j: (i // per_b, 0, 0)),
            pl.BlockSpec((None, 1, d), lambda i, j: (i // per_b, 0, 0)),
            pl.BlockSpec((d, tn), lambda i, j: (0, j)),
        ],
        out_specs=pl.BlockSpec((tm, tn), lambda i, j: (i, j)),
        scratch_shapes=[pltpu.VMEM((tm, d), MXU_DTYPE)],
        compiler_params=_params(("parallel", "arbitrary")),
        name=name,
    )(x2, g.reshape(1, d), shift.reshape(bsz, 1, d), scale.reshape(bsz, 1, d), w)


def _res_matmul_kernel(a_ref, w_ref, x_ref, gt_ref, o_ref):
    o_ref[...] = x_ref[...] + gt_ref[...] * _dot(a_ref[...], w_ref[...])


def _res_matmul(a2, w, x2, seq, gate, name):
    n, k = a2.shape
    d = w.shape[1]
    tm = _row_tile(n, seq, 512)
    tn = 512
    per_b = seq // tm
    bsz = n // seq
    return pl.pallas_call(
        _res_matmul_kernel,
        out_shape=jax.ShapeDtypeStruct((n, d), F32),
        grid=(n // tm, d // tn),
        in_specs=[
            pl.BlockSpec((tm, k), lambda i, j: (i, 0)),
            pl.BlockSpec((k, tn), lambda i, j: (0, j)),
            pl.BlockSpec((tm, tn), lambda i, j: (i, j)),
            pl.BlockSpec((None, 1, tn), lambda i, j: (i // per_b, 0, j)),
        ],
        out_specs=pl.BlockSpec((tm, tn), lambda i, j: (i, j)),
        compiler_params=_params(("parallel", "arbitrary")),
        name=name,
    )(a2, w, x2, gate.reshape(bsz, 1, d))


def _ffn_kernel(x_ref, g_ref, sh_ref, sc_ref, gt_ref, wg_ref, wu_ref, wo_ref, o_ref, h_scr):
    k = pl.program_id(1)

    @pl.when(k == 0)
    def _():
        h_scr[...] = _norm_mod(x_ref[...], g_ref[...], sh_ref[...], sc_ref[...]).astype(h_scr.dtype)

    h = h_scr[...]
    gate = _dot(h, wg_ref[...])
    up = _dot(h, wu_ref[...])
    act = (gate * jax.nn.sigmoid(gate) * up).astype(MXU_DTYPE)
    part = _dot(act, wo_ref[...])

    @pl.when(k == 0)
    def _():
        o_ref[...] = part

    @pl.when(k > 0)
    def _():
        o_ref[...] += part

    @pl.when(k == pl.num_programs(1) - 1)
    def _():
        o_ref[...] = x_ref[...] + gt_ref[...] * o_ref[...]


def _ffn(x2, seq, g, shift, scale, gate, w_in, w_out):
    n, d = x2.shape
    hid = w_out.shape[0]
    tm = _row_tile(n, seq, 512)
    th = 512
    assert hid % th == 0
    nk = hid // th
    per_b = seq // tm
    bsz = n // seq
    bvec = lambda i, k: (i // per_b, 0, 0)
    return pl.pallas_call(
        _ffn_kernel,
        out_shape=jax.ShapeDtypeStruct((n, d), F32),
        grid=(n // tm, nk),
        in_specs=[
            pl.BlockSpec((tm, d), lambda i, k: (i, 0)),
            pl.BlockSpec((1, d), lambda i, k: (0, 0)),
            pl.BlockSpec((None, 1, d), bvec),
            pl.BlockSpec((None, 1, d), bvec),
            pl.BlockSpec((None, 1, d), bvec),
            pl.BlockSpec((d, th), lambda i, k: (0, k)),
            pl.BlockSpec((d, th), lambda i, k: (0, k + nk)),
            pl.BlockSpec((th, d), lambda i, k: (k, 0)),
        ],
        out_specs=pl.BlockSpec((tm, d), lambda i, k: (i, 0)),
        scratch_shapes=[pltpu.VMEM((tm, d), MXU_DTYPE)],
        compiler_params=_params(("parallel", "arbitrary")),
        name="ffn",
    )(x2, g.reshape(1, d), shift.reshape(bsz, 1, d), scale.reshape(bsz, 1, d),
      gate.reshape(bsz, 1, d), w_in, w_in, w_out)


def _rms_kernel(x_ref, g_ref, o_ref):
    x = x_ref[...]
    ms = jnp.mean(x * x, axis=-1, keepdims=True)
    o_ref[...] = x * lax.rsqrt(ms + RMS_EPS) * g_ref[...]


def _final_norm(x2, g):
    n, d = x2.shape
    tm = min(512, n)
    return pl.pallas_call(
        _rms_kernel,
        out_shape=jax.ShapeDtypeStruct((n, d), F32),
        grid=(n // tm,),
        in_specs=[pl.BlockSpec((tm, d), lambda i: (i, 0)),
                  pl.BlockSpec((1, d), lambda i: (0, 0))],
        out_specs=pl.BlockSpec((tm, d), lambda i: (i, 0)),
        compiler_params=_params(("parallel",)),
        name="final_norm",
    )(x2, g.reshape(1, d))


def _gla_kernel(q_ref, k_ref, v_ref, g_ref, al_ref, wup_ref, ba_ref, ng_ref, o_ref,
                st_scr, *, q_scale):
    c = GLA_CHUNK

    @pl.when(pl.program_id(2) == 0)
    def _():
        st_scr[...] = jnp.zeros_like(st_scr)

    rows = lax.broadcasted_iota(jnp.int32, (c, V7X_LANES), 0)
    cols = lax.broadcasted_iota(jnp.int32, (c, V7X_LANES), 1)
    causal = rows >= cols
    cols_sub = lax.broadcasted_iota(jnp.int32, (GLA_SUB, V7X_LANES), 1)
    tri = (lax.broadcasted_iota(jnp.int32, (c, c), 0)
           >= lax.broadcasted_iota(jnp.int32, (c, c), 1)).astype(MXU_DTYPE)

    for ci in range(q_ref.shape[0] // c):
        sl = pl.ds(ci * c, c)
        q = q_ref[sl, :] * q_scale
        k = k_ref[sl, :]
        v = v_ref[sl, :].astype(MXU_DTYPE)
        pre = _dot(al_ref[sl, :].astype(MXU_DTYPE), wup_ref[...]) + ba_ref[...]
        la = -_softplus(-pre) * (1.0 / GLA_TAU)
        b = _split_dot_left(tri, la)
        b_last = b[c - 1:c, :]
        st = st_scr[...]
        inter = lax.dot_general((q * jnp.exp(b)).astype(MXU_DTYPE), st.astype(MXU_DTYPE), _NT,
                                preferred_element_type=F32)

        sub = GLA_SUB
        att_rows = []
        for i in range(c // sub):
            lo_r, hi_r = i * sub, (i + 1) * sub
            q_i, b_i = q[lo_r:hi_r], b[lo_r:hi_r]
            if i == 0:
                att_i = jnp.zeros((sub, V7X_LANES), F32)
            else:
                ref = b[lo_r:lo_r + 1]
                q_f = (q_i * jnp.exp(b_i - ref)).astype(MXU_DTYPE)
                k_f = k[:lo_r] * jnp.exp(ref - b[:lo_r])
                k_f = jnp.concatenate([k_f, jnp.zeros((V7X_LANES - lo_r, k_f.shape[1]), F32)],
                                      axis=0).astype(MXU_DTYPE)
                att_i = lax.dot_general(q_f, k_f, _NT, preferred_element_type=F32)
            for j in range(sub):
                s_idx = lo_r + j
                e = jnp.exp(jnp.minimum(b_i - b[s_idx:s_idx + 1], 0.0))
                col = jnp.sum(q_i * k[s_idx:s_idx + 1] * e, axis=1, keepdims=True)
                att_i = jnp.where(cols_sub == s_idx, col, att_i)
            att_rows.append(att_i)
        att = jnp.where(causal, jnp.concatenate(att_rows, axis=0), 0.0)
        v_pad = jnp.concatenate([v_ref[sl, :], jnp.zeros((V7X_LANES - c, v.shape[1]), F32)],
                                axis=0).astype(MXU_DTYPE)
        o = inter + _dot(att.astype(MXU_DTYPE), v_pad)

        kd = (k * jnp.exp(b_last - b)).astype(MXU_DTYPE)
        st_scr[...] = jnp.exp(b_last) * st + lax.dot_general(v, kd, _TN, preferred_element_type=F32)

        ms = jnp.mean(o * o, axis=-1, keepdims=True)
        on = o * lax.rsqrt(ms + RMS_EPS) * ng_ref[...]
        gg = g_ref[sl, :]
        o_ref[sl, :] = (on * (gg * jax.nn.sigmoid(gg))).astype(o_ref.dtype)


def _split_dot_left(m, x):
    hi = x.astype(MXU_DTYPE)
    lo = (x - hi.astype(F32)).astype(MXU_DTYPE)
    return _dot(m, hi) + _dot(m, lo)


def _gla_scan(proj, alow, w_up, b_alpha, norm_g, bsz, seq):
    dk = w_up.shape[1]
    hk = dk // GLA_HEADS
    dv = proj.shape[-1] // 2 - dk
    hv = dv // GLA_HEADS
    tb = min(256, seq)
    proj3 = proj.reshape(bsz, seq, proj.shape[-1])
    alow3 = alow.reshape(bsz, seq, alow.shape[-1])
    kern = functools.partial(_gla_kernel, q_scale=hk ** -0.5)
    return pl.pallas_call(
        kern,
        out_shape=jax.ShapeDtypeStruct((bsz, seq, dv), MXU_DTYPE),
        grid=(bsz, GLA_HEADS, seq // tb),
        in_specs=[
            pl.BlockSpec((None, tb, hk), lambda b, h, t: (b, t, h)),
            pl.BlockSpec((None, tb, hk), lambda b, h, t: (b, t, GLA_HEADS + h)),
            pl.BlockSpec((None, tb, hv), lambda b, h, t: (b, t, (2 * dk) // hv + h)),
            pl.BlockSpec((None, tb, hv), lambda b, h, t: (b, t, (2 * dk + dv) // hv + h)),
            pl.BlockSpec((None, tb, alow3.shape[-1]), lambda b, h, t: (b, t, 0)),
            pl.BlockSpec((w_up.shape[0], hk), lambda b, h, t: (0, h)),
            pl.BlockSpec((1, hk), lambda b, h, t: (0, h)),
            pl.BlockSpec((1, hv), lambda b, h, t: (0, 0)),
        ],
        out_specs=pl.BlockSpec((None, tb, hv), lambda b, h, t: (b, t, h)),
        scratch_shapes=[pltpu.VMEM((hv, hk), F32)],
        compiler_params=_params(("parallel", "parallel", "arbitrary")),
        name="gla_scan",
    )(proj3, proj3, proj3, proj3, alow3, w_up, b_alpha.reshape(1, dk), norm_g.reshape(1, hv))


def _gla_layer(x2, bsz, seq, norm_g_mix, shift, scale, gate, w_in, w_alpha_up, b_alpha, norm_g, w_out):
    d = x2.shape[1]
    dk = w_alpha_up.shape[1]
    main = 2 * dk + 2 * d
    w_main = w_in[:, :main].astype(MXU_DTYPE)
    w_low = jnp.pad(w_in[:, main:], ((0, 0), (0, V7X_LANES - GLA_LOWRANK))).astype(MXU_DTYPE)
    w_up = jnp.pad(w_alpha_up, ((0, V7X_LANES - GLA_LOWRANK), (0, 0))).astype(MXU_DTYPE)
    proj = _nm_matmul(x2, seq, norm_g_mix, shift, scale, w_main, F32, 512, "gla_proj")
    alow = _nm_matmul(x2, seq, norm_g_mix, shift, scale, w_low, F32, V7X_LANES, "gla_lowrank")
    og = _gla_scan(proj, alow, w_up, b_alpha, norm_g, bsz, seq)
    return _res_matmul(og.reshape(bsz * seq, d), w_out.astype(MXU_DTYPE), x2, seq, gate, "gla_out")


def _sb_kernel(q_ref, k_ref, v_ref, o_ref, acc_scr, run_scr, *, scale):
    tq, tk = SB_QTILE, SB_KTILE
    hq = tq // SB_ROW_SPLIT
    qi = pl.program_id(2)
    kr = lax.broadcasted_iota(jnp.int32, (tk, tk), 0)
    kc = lax.broadcasted_iota(jnp.int32, (tk, tk), 1)
    later_m = (kr > kc).astype(MXU_DTYPE)
    later_m2 = jnp.concatenate([later_m, later_m], axis=0)
    q_pos = qi * tq + lax.broadcasted_iota(jnp.int32, (hq, tk), 0)
    k_off = lax.broadcasted_iota(jnp.int32, (hq, tk), 1)

    def tile(kj, masked):
        start = pl.multiple_of(kj * tk, tk)
        ks = k_ref[pl.ds(start, tk), :]
        vs = v_ref[pl.ds(start, tk), :]
        zs = [lax.dot_general(q_ref[h * hq:(h + 1) * hq, :], ks, _NT, preferred_element_type=F32) * scale
              for h in range(SB_ROW_SPLIT)]
        sps = []
        for h, z in enumerate(zs):
            sp = jnp.maximum(z, 0.0) + jnp.log(1.0 + jnp.exp(-jnp.abs(z)))
            if masked:
                sp = jnp.where(start + k_off < q_pos + h * hq, sp, 0.0)
            sps.append(sp)
        laters = []
        for sp in sps:
            hi = sp.astype(MXU_DTYPE)
            lo = (sp - hi.astype(F32)).astype(MXU_DTYPE)
            laters.append(_dot(jnp.concatenate([hi, lo], axis=1), later_m2))
        for h in range(SB_ROW_SPLIT):
            rs = slice(h * hq, (h + 1) * hq)
            run = run_scr[rs, :]
            w = jnp.exp(zs[h] - sps[h] - laters[h] - jnp.concatenate([run] * (tk // V7X_LANES), axis=1))
            if masked:
                w = jnp.where(start + k_off < q_pos + h * hq, w, 0.0)
            acc_scr[rs, :] += _dot(w.astype(MXU_DTYPE), vs)
            run_scr[rs, :] = run + laters[h][:, 0:1] + sps[h][:, 0:1]

    acc_scr[...] = jnp.zeros_like(acc_scr)
    run_scr[...] = jnp.zeros_like(run_scr)
    per_q = tq // tk
    for m in range(per_q):
        tile(qi * per_q + (per_q - 1 - m), True)

    def body(m, carry):
        tile(qi * per_q - 1 - m, False)
        return carry

    lax.fori_loop(0, qi * per_q, body, 0)
    o_ref[...] = acc_scr[...].astype(o_ref.dtype)


def _sb_attention(qkv, bsz, seq):
    d = qkv.shape[-1] // 3
    dh = d // SB_HEADS
    assert seq % SB_QTILE == 0 and SB_QTILE % SB_KTILE == 0
    qkv3 = qkv.reshape(bsz, seq, 3 * d)
    kern = functools.partial(_sb_kernel, scale=dh ** -0.5)
    return pl.pallas_call(
        kern,
        out_shape=jax.ShapeDtypeStruct((bsz, seq, d), MXU_DTYPE),
        grid=(bsz, SB_HEADS, seq // SB_QTILE),
        in_specs=[
            pl.BlockSpec((None, SB_QTILE, dh), lambda b, h, i: (b, i, h)),
            pl.BlockSpec((None, seq, dh), lambda b, h, i: (b, 0, SB_HEADS + h)),
            pl.BlockSpec((None, seq, dh), lambda b, h, i: (b, 0, 2 * SB_HEADS + h)),
        ],
        out_specs=pl.BlockSpec((None, SB_QTILE, dh), lambda b, h, i: (b, i, h)),
        scratch_shapes=[pltpu.VMEM((SB_QTILE, dh), F32), pltpu.VMEM((SB_QTILE, V7X_LANES), F32)],
        compiler_params=_params(("parallel", "parallel", "arbitrary")),
        name="sb_attention",
    )(qkv3, qkv3, qkv3)


def _sb_layer(x2, bsz, seq, norm_g_mix, shift, scale, gate, w_in, w_out):
    d = x2.shape[1]
    qkv = _nm_matmul(x2, seq, norm_g_mix, shift, scale, w_in.astype(MXU_DTYPE), MXU_DTYPE, 512, "sb_proj")
    o = _sb_attention(qkv, bsz, seq)
    return _res_matmul(o.reshape(bsz * seq, d), w_out.astype(MXU_DTYPE), x2, seq, gate, "sb_out")


def _rw_proj_kernel(mix_ids_ref, x_ref, xp_ref, g_ref, sh_ref, sc_ref, mu_ref, w_ref, o_ref, mix_scr,
                    *, tiles_per_seq):
    i = pl.program_id(0)
    j = pl.program_id(1)

    @pl.when(j == 0)
    def _():
        g = g_ref[...]
        sh = sh_ref[...]
        sc = sc_ref[...]
        h = _norm_mod(x_ref[...], g, sh, sc)
        hp = _norm_mod(xp_ref[...], g, sh, sc)[7:8, :]
        hp = jnp.where(i % tiles_per_seq == 0, 0.0, hp)
        rows = lax.broadcasted_iota(jnp.int32, h.shape, 0)
        prev = jnp.where(rows == 0, hp, pltpu.roll(h, 1, 0))
        xx = prev - h
        for n in range(mix_scr.shape[0]):
            mix_scr[n] = (h + xx * mu_ref[n:n + 1, :]).astype(mix_scr.dtype)

    o_ref[...] = _dot(mix_scr[mix_ids_ref[j]], w_ref[...])


def _rw_proj(x2, seq, g, shift, scale, mu, w_all, mix_ids, tn):
    n, d = x2.shape
    n_out = w_all.shape[1]
    tm = _row_tile(n, seq, 512)
    per_b = seq // tm
    bsz = n // seq
    kern = functools.partial(_rw_proj_kernel, tiles_per_seq=per_b)
    bvec = lambda i, j, ids: (i // per_b, 0, 0)
    return pl.pallas_call(
        kern,
        out_shape=jax.ShapeDtypeStruct((n, n_out), F32),
        grid_spec=pltpu.PrefetchScalarGridSpec(
            num_scalar_prefetch=1,
            grid=(n // tm, n_out // tn),
            in_specs=[
                pl.BlockSpec((tm, d), lambda i, j, ids: (i, 0)),
                pl.BlockSpec((8, d), lambda i, j, ids: (jnp.maximum(i * (tm // 8) - 1, 0), 0)),
                pl.BlockSpec((1, d), lambda i, j, ids: (0, 0)),
                pl.BlockSpec((None, 1, d), bvec),
                pl.BlockSpec((None, 1, d), bvec),
                pl.BlockSpec((8, d), lambda i, j, ids: (0, 0)),
                pl.BlockSpec((d, tn), lambda i, j, ids: (0, j)),
            ],
            out_specs=pl.BlockSpec((tm, tn), lambda i, j, ids: (i, j)),
            scratch_shapes=[pltpu.VMEM((6, tm, d), MXU_DTYPE)],
        ),
        compiler_params=_params(("parallel", "arbitrary")),
        name="rwkv_proj",
    )(mix_ids, x2, x2, g.reshape(1, d), shift.reshape(bsz, 1, d), scale.reshape(bsz, 1, d),
      jnp.pad(mu, ((0, 2), (0, 0))), w_all)


def _lora_up_kernel(p_ref, w_ref, o_ref, *, act):
    p = p_ref[...]
    if act == "tanh":
        p = jnp.tanh(p)
    elif act == "sigmoid":
        p = jax.nn.sigmoid(p)
    o_ref[...] = _dot(p.astype(MXU_DTYPE), w_ref[...])


def _lora_up(proj, col_block, w2, act, tn_in, name):
    n = proj.shape[0]
    d = w2.shape[1]
    tm = min(512, n)
    kern = functools.partial(_lora_up_kernel, act=act)
    return pl.pallas_call(
        kern,
        out_shape=jax.ShapeDtypeStruct((n, d), F32),
        grid=(n // tm,),
        in_specs=[pl.BlockSpec((tm, tn_in), lambda i: (i, col_block)),
                  pl.BlockSpec((tn_in, d), lambda i: (0, 0))],
        out_specs=pl.BlockSpec((tm, d), lambda i: (i, 0)),
        compiler_params=_params(("parallel",)),
        name=name,
    )(proj, w2)


def _rw_scan_kernel(r_ref, k_ref, v_ref, wp_ref, ap_ref, gt_ref,
                    w0_ref, a0_ref, kk_ref, ka_ref, rk_ref, gng_ref, gnb_ref, o_ref,
                    s_scr, rows_scr, yt_scr):
    hd = RW_HEAD_DIM
    tb, gl = r_ref.shape
    n_pairs = gl // V7X_LANES
    half = tb // 2

    @pl.when(pl.program_id(2) == 0)
    def _():
        s_scr[...] = jnp.zeros_like(s_scr)

    yt_scr[...] = jnp.zeros_like(yt_scr)

    li = lax.broadcasted_iota(jnp.int32, (V7X_LANES, V7X_LANES), 0) // hd
    lj = lax.broadcasted_iota(jnp.int32, (V7X_LANES, V7X_LANES), 1) // hd
    pair_ones = (li == lj).astype(MXU_DTYPE)
    pair_ones2 = jnp.concatenate([pair_ones, pair_ones], axis=0)

    def head_sum(x):
        return jnp.concatenate(
            [_split_dot(x[:, c * V7X_LANES:(c + 1) * V7X_LANES], pair_ones) for c in range(n_pairs)], axis=1)

    r = r_ref[...]
    k = k_ref[...]
    v = v_ref[...]
    w_log = -_softplus(-(w0_ref[...] + wp_ref[...])) - 0.5
    decay = jnp.exp(-jnp.exp(w_log))
    a = jax.nn.sigmoid(a0_ref[...] + ap_ref[...])
    kk = k * kk_ref[...]
    kk = kk / jnp.maximum(jnp.sqrt(head_sum(kk * kk)), 1e-12)
    k2 = k * (1.0 + (a - 1.0) * ka_ref[...])
    v_hi = v.astype(MXU_DTYPE).astype(F32)

    for n, val in enumerate((-kk, decay, kk * a, k2, r, v_hi, v - v_hi)):
        for p in range(n_pairs):
            rows_scr[n, p] = val[:, p * V7X_LANES:(p + 1) * V7X_LANES]

    rows64 = lax.broadcasted_iota(jnp.int32, (hd, V7X_LANES), 0)
    lanes64 = lax.broadcasted_iota(jnp.int32, (hd, V7X_LANES), 1) % hd
    eye2 = (rows64 == lanes64).astype(F32)

    def split_cat(x):
        hi = x.astype(MXU_DTYPE)
        lo = (x - hi.astype(F32)).astype(MXU_DTYPE)
        return jnp.concatenate([hi, lo], axis=1)

    shift = half.bit_length() - 1

    def put_y(p, u_plus, y):
        slot = 3 * p + lax.shift_right_logical(u_plus, shift)
        yt_scr[slot] = jnp.where(lanes64 == lax.bitwise_and(u_plus, half - 1), y, yt_scr[slot])

    def step(t, carry):
        t_prev = jnp.maximum(t - 1, 0)
        reds = []
        for p in range(n_pairs):
            row = lambda n, at=t: rows_scr[n, p, pl.ds(at, 1), :]
            s = s_scr[p]
            v_diag = jnp.concatenate([(eye2 * row(5)).astype(MXU_DTYPE),
                                      (eye2 * row(6)).astype(MXU_DTYPE)], axis=1)
            lhs = jnp.concatenate([split_cat(s * row(0)), split_cat(s * row(4, t_prev)), v_diag], axis=0)
            reds.append(_dot(lhs, pair_ones2))
        for p in range(n_pairs):
            row = lambda n: rows_scr[n, p, pl.ds(t, 1), :]
            sa = reds[p][:hd]
            v_col = reds[p][2 * hd:]
            s_scr[p] = s_scr[p] * row(1) + sa * row(2) + v_col * row(3)
            put_y(p, t + (half - 1), reds[p][hd:2 * hd])
        return carry

    lax.fori_loop(0, tb, step, 0)
    for p in range(n_pairs):
        y_last = _dot(split_cat(s_scr[p] * rows_scr[4, p, tb - 1:tb, :]), pair_ones2)
        put_y(p, jnp.int32(tb - 1 + half), y_last)

    def pair_rows(p):
        m = jnp.concatenate([yt_scr[3 * p + 1], yt_scr[3 * p + 2]], axis=0).T
        top = jnp.concatenate([m[:hd, :hd], m[hd:, :hd]], axis=1)
        bot = jnp.concatenate([m[:hd, hd:], m[hd:, hd:]], axis=1)
        return jnp.concatenate([top, bot], axis=0)

    y = jnp.concatenate([pair_rows(p) for p in range(n_pairs)], axis=1)

    inv = 1.0 / hd
    mean = head_sum(y) * inv
    dlt = y - mean
    var = head_sum(dlt * dlt) * inv
    yn = dlt * lax.rsqrt(var + RW_GN_EPS) * gng_ref[...] + gnb_ref[...]
    bonus = head_sum(r * k2 * rk_ref[...]) * v
    o_ref[...] = ((yn + bonus) * gt_ref[...]).astype(o_ref.dtype)


def _rw_scan(proj, w_pre, a_pre, g_out, w0, a0, k_k, k_a, r_k, gn_g, gn_b, bsz, seq, d):
    gl = RW_GROUP * RW_HEAD_DIM
    assert seq % RW_TBLK == 0 and d % gl == 0 and gl % V7X_LANES == 0
    n_grp = d // gl
    n_pairs = gl // V7X_LANES
    p3 = proj.reshape(bsz, seq, proj.shape[-1])
    act = lambda off: pl.BlockSpec((None, RW_TBLK, gl), lambda b, g, t: (b, t, off + g))
    vec = pl.BlockSpec((1, gl), lambda b, g, t: (0, g))
    as3 = lambda z: z.reshape(bsz, seq, d)
    as_row = lambda z: z.reshape(1, d)
    return pl.pallas_call(
        _rw_scan_kernel,
        out_shape=jax.ShapeDtypeStruct((bsz, seq, d), MXU_DTYPE),
        grid=(bsz, n_grp, seq // RW_TBLK),
        in_specs=[act(0), act(n_grp), act(2 * n_grp), act(0), act(0), act(0)] + [vec] * 7,
        out_specs=pl.BlockSpec((None, RW_TBLK, gl), lambda b, g, t: (b, t, g)),
        scratch_shapes=[
            pltpu.VMEM((n_pairs, RW_HEAD_DIM, V7X_LANES), F32),
            pltpu.VMEM((7, n_pairs, RW_TBLK, V7X_LANES), F32),
            pltpu.VMEM((3 * n_pairs, RW_HEAD_DIM, V7X_LANES), F32),
        ],
        compiler_params=_params(("parallel", "parallel", "arbitrary")),
        name="rwkv_scan",
    )(p3, p3, p3, as3(w_pre), as3(a_pre), as3(g_out),
      as_row(w0), as_row(a0), as_row(k_k), as_row(k_a), as_row(r_k), as_row(gn_g), as_row(gn_b))


def _rw_chunk_kernel(r_ref, k_ref, v_ref, wp_ref, ap_ref, gt_ref,
                     w0_ref, a0_ref, kk_ref, ka_ref, rk_ref, gng_ref, gnb_ref, o_ref,
                     zt_scr, ops_scr, y_scr):
    hd = RW_HEAD_DIM
    cs = RW_CHUNK
    qw = RW_QUAD * hd
    tb, d = r_ref.shape
    n_quads = d // qw
    n_pairs = d // V7X_LANES

    @pl.when(pl.program_id(1) == 0)
    def _():
        zt_scr[...] = jnp.zeros_like(zt_scr)

    def iota(shape, dim):
        return lax.broadcasted_iota(jnp.int32, shape, dim)

    def shr(x, n):
        return lax.shift_right_logical(x, n)

    lg_hd, lg_cs = hd.bit_length() - 1, cs.bit_length() - 1
    li, lj = shr(iota((V7X_LANES, V7X_LANES), 0), lg_hd), shr(iota((V7X_LANES, V7X_LANES), 1), lg_hd)
    pair_ones = (li == lj).astype(MXU_DTYPE)

    def head_sum(x):
        return jnp.concatenate(
            [_split_dot(x[:, c * V7X_LANES:(c + 1) * V7X_LANES], pair_ones) for c in range(n_pairs)], axis=1)

    r = r_ref[...]
    k = k_ref[...]
    v = v_ref[...]
    w_log = -_softplus(-(w0_ref[...] + wp_ref[...])) - 0.5
    a = jax.nn.sigmoid(a0_ref[...] + ap_ref[...])
    kk = k * kk_ref[...]
    kk = kk / jnp.maximum(jnp.sqrt(head_sum(kk * kk)), 1e-12)
    k2 = k * (1.0 + (a - 1.0) * ka_ref[...])
    for n, val in enumerate((-jnp.exp(w_log), -kk, kk * a, k2, r, v)):
        ops_scr[n] = val

    nb = RW_QUAD * cs
    tri = (iota((cs, cs), 0) >= iota((cs, cs), 1)).astype(MXU_DTYPE)
    same_head_rows = shr(iota((nb, qw), 0), lg_cs) == shr(iota((nb, qw), 1), lg_hd)
    col = iota((cs, V7X_LANES), 1)
    row = iota((cs, V7X_LANES), 0)
    col_s = lax.bitwise_and(col, cs - 1)
    m_ak = (col >= nb) & (col_s < row)
    m_n = col_s <= row
    lr, lc = iota((nb, nb), 0), iota((nb, nb), 1)
    m_lbd = (shr(lr, lg_cs) == shr(lc, lg_cs)) & (lax.bitwise_and(lc, cs - 1) < lax.bitwise_and(lr, cs - 1))
    eye = (lr == lc).astype(F32)
    m_state = shr(iota((qw, qw), 0), lg_hd) == shr(iota((qw, qw), 1), lg_hd)

    def mm(x, y, dims=(((1,), (0,)), ((), ()))):
        return lax.dot_general(x.astype(MXU_DTYPE), y.astype(MXU_DTYPE), dims, preferred_element_type=F32)

    def stack4(x):
        return jnp.where(same_head_rows, jnp.concatenate([x] * RW_QUAD, axis=0), 0.0)

    def chunk(c, carry):
        sl = pl.ds(pl.multiple_of(c * cs, cs), cs)
        lw = ops_scr[0, sl, :]
        cl = _split_dot_left(tri, lw)
        g_in = jnp.exp(cl)
        g_inv = jnp.exp(-cl)
        at_all = ops_scr[1, sl, :] * jnp.exp(cl - lw)
        bt_all = ops_scr[2, sl, :] * g_inv
        kt_all = ops_scr[3, sl, :] * g_inv
        rt_all = ops_scr[4, sl, :] * g_in
        v_all = ops_scr[5, sl, :]
        g_last = g_in[cs - 1:cs, :]
        quads = range(n_quads)
        qs = lambda x, q: x[:, q * qw:(q + 1) * qw]

        lhs = [jnp.concatenate([qs(at_all, q), qs(rt_all, q)], axis=0) for q in quads]
        vbd = [stack4(qs(v_all, q)) for q in quads]
        gram = [mm(lhs[q], jnp.concatenate([stack4(qs(bt_all, q)), stack4(qs(kt_all, q))], axis=0), _NT)
                for q in quads]
        l1 = [jnp.where(m_lbd, jnp.concatenate([gram[q][:cs, :nb]] * RW_QUAD, axis=0), 0.0)
              for q in quads]
        lak_v = [mm(jnp.where(m_ak, gram[q][:cs], 0.0), jnp.concatenate([vbd[q], vbd[q]], axis=0))
                 for q in quads]
        inv = [eye + l1[q] for q in quads]
        pw = l1
        for _ in range(lg_cs - 1):
            pw = [mm(pw[q], pw[q]) for q in quads]
            inv = [inv[q] + mm(inv[q], pw[q]) for q in quads]

        az = [mm(lhs[q], zt_scr[q], _NT) for q in quads]
        ubig = [mm(inv[q], stack4(az[q][:cs] + lak_v[q])) for q in quads]
        u = [sum(ubig[q][h * cs:(h + 1) * cs] for h in range(RW_QUAD)) for q in quads]
        y = [az[q][cs:] + mm(jnp.where(m_n, gram[q][cs:], 0.0),
                             jnp.concatenate([stack4(u[q]), vbd[q]], axis=0)) for q in quads]
        for q in quads:
            upd = mm(jnp.concatenate([u[q], qs(v_all, q)], axis=0),
                     jnp.concatenate([qs(bt_all, q), qs(kt_all, q)], axis=0), _TN)
            zt_scr[q] = (zt_scr[q] + jnp.where(m_state, upd, 0.0)) * qs(g_last, q)
            y_scr[sl, q * qw:(q + 1) * qw] = y[q]
        return carry

    lax.fori_loop(0, tb // cs, chunk, 0)

    y = y_scr[...]
    inv_hd = 1.0 / hd
    mean = head_sum(y) * inv_hd
    dlt = y - mean
    var = head_sum(dlt * dlt) * inv_hd
    yn = dlt * lax.rsqrt(var + RW_GN_EPS) * gng_ref[...] + gnb_ref[...]
    bonus = head_sum(r * k2 * rk_ref[...]) * v
    o_ref[...] = ((yn + bonus) * gt_ref[...]).astype(o_ref.dtype)


def _rw_chunk_scan(proj, w_pre, a_pre, g_out, w0, a0, k_k, k_a, r_k, gn_g, gn_b, bsz, seq, d):
    qw = RW_QUAD * RW_HEAD_DIM
    assert seq % RW_TBLK == 0 and RW_TBLK % RW_CHUNK == 0 and d % qw == 0
    assert 2 * RW_QUAD * RW_CHUNK == V7X_LANES
    n_quads = d // qw
    p3 = proj.reshape(bsz, seq, proj.shape[-1])
    act = lambda off: pl.BlockSpec((None, RW_TBLK, d), lambda b, t: (b, t, off))
    vec = pl.BlockSpec((1, d), lambda b, t: (0, 0))
    as3 = lambda z: z.reshape(bsz, seq, d)
    as_row = lambda z: z.reshape(1, d)
    return pl.pallas_call(
        _rw_chunk_kernel,
        out_shape=jax.ShapeDtypeStruct((bsz, seq, d), MXU_DTYPE),
        grid=(bsz, seq // RW_TBLK),
        in_specs=[act(0), act(1), act(2), act(0), act(0), act(0)] + [vec] * 7,
        out_specs=pl.BlockSpec((None, RW_TBLK, d), lambda b, t: (b, t, 0)),
        scratch_shapes=[
            pltpu.VMEM((n_quads, qw, qw), F32),
            pltpu.VMEM((6, RW_TBLK, d), F32),
            pltpu.VMEM((RW_TBLK, d), F32),
        ],
        compiler_params=_params(("parallel", "arbitrary")),
        name="rwkv_scan",
    )(p3, p3, p3, as3(w_pre), as3(a_pre), as3(g_out),
      as_row(w0), as_row(a0), as_row(k_k), as_row(k_a), as_row(r_k), as_row(gn_g), as_row(gn_b))


def _rw_layer(x2, bsz, seq, norm_g_mix, shift, scale, gate, mu, w_rkv, w0, w1, w2, a0, a1, a2, g1, g2,
              k_k, k_a, r_k, gn_g, gn_b, w_out):
    d = x2.shape[1]
    tn = 512
    pad_c = lambda m: jnp.pad(m, ((0, 0), (0, tn - m.shape[1])))
    pad_r = lambda m: jnp.pad(m, ((0, tn - m.shape[0]), (0, 0))).astype(MXU_DTYPE)
    w_all = jnp.concatenate([w_rkv[0], w_rkv[1], w_rkv[2], pad_c(w1), pad_c(a1), pad_c(g1)],
                            axis=1).astype(MXU_DTYPE)
    per = d // tn
    mix_ids = jnp.array([0] * per + [2] * per + [3] * per + [1, 4, 5], jnp.int32)
    proj = _rw_proj(x2, seq, norm_g_mix, shift, scale, mu, w_all, mix_ids, tn)
    w_pre = _lora_up(proj, 3 * per, pad_r(w2), "tanh", tn, "rwkv_decay_lora")
    a_pre = _lora_up(proj, 3 * per + 1, pad_r(a2), "none", tn, "rwkv_a_lora")
    g_out = _lora_up(proj, 3 * per + 2, pad_r(g2), "sigmoid", tn, "rwkv_gate_lora")
    yg = _rw_chunk_scan(proj, w_pre, a_pre, g_out, w0, a0, k_k, k_a, r_k, gn_g, gn_b, bsz, seq, d)
    return _res_matmul(yg.reshape(bsz * seq, d), w_out.astype(MXU_DTYPE), x2, seq, gate, "rwkv_out")


def kernel(x, c, norm_mix_g, norm_ffn_g, w_mod, b_mod, ffn_w_in, ffn_w_out, final_g, gla_w_in, gla_w_alpha_up, gla_b_alpha, gla_norm_g, gla_w_out, sb_w_in, sb_w_out, rw_mu, rw_w_rkv, rw_w0, rw_w1, rw_w2, rw_a0, rw_a1, rw_a2, rw_g1, rw_g2, rw_k_k, rw_k_a, rw_r_k, rw_gn_g, rw_gn_b, rw_w_out):
    bsz, seq, d = x.shape
    depth = w_mod.shape[0]
    mod = _mod_all(c, w_mod, b_mod)
    x2 = x.reshape(bsz * seq, d)
    for i in range(depth):
        sh_m, sc_m, gt_m, sh_f, sc_f, gt_f = (mod[i, :, n * d:(n + 1) * d] for n in range(6))
        kind, j = i % 3, i // 3
        if kind == 0:
            x2 = _gla_layer(x2, bsz, seq, norm_mix_g[i], sh_m, sc_m, gt_m, gla_w_in[j],
                            gla_w_alpha_up[j], gla_b_alpha[j], gla_norm_g[j], gla_w_out[j])
        elif kind == 1:
            x2 = _sb_layer(x2, bsz, seq, norm_mix_g[i], sh_m, sc_m, gt_m, sb_w_in[j], sb_w_out[j])
        else:
            x2 = _rw_layer(x2, bsz, seq, norm_mix_g[i], sh_m, sc_m, gt_m, rw_mu[j], rw_w_rkv[j],
                           rw_w0[j], rw_w1[j], rw_w2[j], rw_a0[j], rw_a1[j], rw_a2[j], rw_g1[j],
                           rw_g2[j], rw_k_k[j], rw_k_a[j], rw_r_k[j], rw_gn_g[j], rw_gn_b[j],
                           rw_w_out[j])
        x2 = _ffn(x2, seq, norm_ffn_g[i], sh_f, sc_f, gt_f,
                  ffn_w_in[i].astype(MXU_DTYPE), ffn_w_out[i].astype(MXU_DTYPE))
    return _final_norm(x2, final_g).reshape(bsz, seq, d)
```

```python
import functools

import jax
import jax.numpy as jnp
from jax import lax
from jax.experimental import pallas as pl
from jax.experimental.pallas import tpu as pltpu

F32 = jnp.float32
MXU_DTYPE = jnp.bfloat16

V7X_LANES = 128
V7X_VMEM_LIMIT_BYTES = 56 * 1024 * 1024

RMS_EPS = 1e-6
LOG2_E = 1.4426950408889634
GLA_HEADS = 4
GLA_LOWRANK = 16
GLA_TAU = 16.0
GLA_CHUNK = 64
GLA_SUB = 16
SB_HEADS = 16
SB_QTILE = 512
SB_KTILE = 256
SB_ROW_SPLIT = 2
RW_HEAD_DIM = 64
RW_GN_EPS = 64e-5
RW_TBLK = 128
RW_CHUNK = 16
RW_QUAD = 4

_NT = (((1,), (1,)), ((), ()))
_TN = (((0,), (0,)), ((), ()))


def _params(semantics):
    return pltpu.CompilerParams(dimension_semantics=semantics,
                                vmem_limit_bytes=V7X_VMEM_LIMIT_BYTES)


def _row_tile(n_rows, seq, want):
    t = min(want, seq)
    assert seq % t == 0 and n_rows % t == 0
    return t


def _dot(a, b):
    return jnp.dot(a, b, preferred_element_type=F32)


def _split_dot(x, m):
    hi = x.astype(MXU_DTYPE)
    lo = (x - hi.astype(F32)).astype(MXU_DTYPE)
    return _dot(hi, m) + _dot(lo, m)


def _softplus(z):
    return jnp.maximum(z, 0.0) + jnp.log1p(jnp.exp(-jnp.abs(z)))


def _norm_mod(x, g, shift, scale):
    ms = jnp.mean(x * x, axis=-1, keepdims=True)
    y = x * lax.rsqrt(ms + RMS_EPS) * g
    return y * (1.0 + scale) + shift


def _mod_kernel(c_ref, w_ref, b_ref, o_ref):
    c = c_ref[...]
    ca = (c * jax.nn.sigmoid(c)).astype(MXU_DTYPE)
    o_ref[...] = _dot(ca, w_ref[...].astype(MXU_DTYPE)) + b_ref[...]


def _mod_all(c, w_mod, b_mod):
    depth, d, six_d = w_mod.shape
    bsz = c.shape[0]
    bp = -(-bsz // 8) * 8
    cp = jnp.pad(c, ((0, bp - bsz), (0, 0)))
    tn = 1024
    out = pl.pallas_call(
        _mod_kernel,
        out_shape=jax.ShapeDtypeStruct((depth, bp, six_d), F32),
        grid=(depth, six_d // tn),
        in_specs=[
            pl.BlockSpec((bp, d), lambda l, j: (0, 0)),
            pl.BlockSpec((None, d, tn), lambda l, j: (l, 0, j)),
            pl.BlockSpec((None, 1, tn), lambda l, j: (l, 0, j)),
        ],
        out_specs=pl.BlockSpec((None, bp, tn), lambda l, j: (l, 0, j)),
        compiler_params=_params(("arbitrary", "arbitrary")),
        name="mod_proj",
    )(cp, w_mod, b_mod.reshape(depth, 1, six_d))
    return out[:, :bsz]


def _nm_matmul_kernel(x_ref, g_ref, sh_ref, sc_ref, w_ref, o_ref, h_scr):
    @pl.when(pl.program_id(1) == 0)
    def _():
        h_scr[...] = _norm_mod(x_ref[...], g_ref[...], sh_ref[...], sc_ref[...]).astype(h_scr.dtype)

    o_ref[...] = _dot(h_scr[...], w_ref[...]).astype(o_ref.dtype)


def _nm_matmul(x2, seq, g, shift, scale, w, out_dtype, tn, name):
    n, d = x2.shape
    n_out = w.shape[1]
    tm = _row_tile(n, seq, 512)
    per_b = seq // tm
    bsz = n // seq
    return pl.pallas_call(
        _nm_matmul_kernel,
        out_shape=jax.ShapeDtypeStruct((n, n_out), out_dtype),
        grid=(n // tm, n_out // tn),
        in_specs=[
            pl.BlockSpec((tm, d), lambda i, j: (i, 0)),
            pl.BlockSpec((1, d), lambda i, j: (0, 0)),
            pl.BlockSpec((None, 1, d), lambda i, j: (i // per_b, 0, 0)),
            pl.BlockSpec((None, 1, d), lambda i, j: (i // per_b, 0, 0)),
            pl.BlockSpec((d, tn), lambda i, j: (0, j)),
        ],
        out_specs=pl.BlockSpec((tm, tn), lambda i, j: (i, j)),
        scratch_shapes=[pltpu.VMEM((tm, d), MXU_DTYPE)],
        compiler_params=_params(("parallel", "arbitrary")),
        name=name,
    )(x2, g.reshape(1, d), shift.reshape(bsz, 1, d), scale.reshape(bsz, 1, d), w)


def _res_matmul_kernel(a_ref, w_ref, x_ref, gt_ref, o_ref):
    o_ref[...] = x_ref[...] + gt_ref[...] * _dot(a_ref[...], w_ref[...])


def _res_matmul(a2, w, x2, seq, gate, name):
    n, k = a2.shape
    d = w.shape[1]
    tm = _row_tile(n, seq, 512)
    tn = 512
    per_b = seq // tm
    bsz = n // seq
    return pl.pallas_call(
        _res_matmul_kernel,
        out_shape=jax.ShapeDtypeStruct((n, d), F32),
        grid=(n // tm, d // tn),
        in_specs=[
            pl.BlockSpec((tm, k), lambda i, j: (i, 0)),
            pl.BlockSpec((k, tn), lambda i, j: (0, j)),
            pl.BlockSpec((tm, tn), lambda i, j: (i, j)),
            pl.BlockSpec((None, 1, tn), lambda i, j: (i // per_b, 0, j)),
        ],
        out_specs=pl.BlockSpec((tm, tn), lambda i, j: (i, j)),
        compiler_params=_params(("parallel", "arbitrary")),
        name=name,
    )(a2, w, x2, gate.reshape(bsz, 1, d))


def _ffn_kernel(x_ref, g_ref, sh_ref, sc_ref, gt_ref, wg_ref, wu_ref, wo_ref, o_ref, h_scr):
    k = pl.program_id(1)

    @pl.when(k == 0)
    def _():
        h_scr[...] = _norm_mod(x_ref[...], g_ref[...], sh_ref[...], sc_ref[...]).astype(h_scr.dtype)

    h = h_scr[...]
    gate = _dot(h, wg_ref[...])
    up = _dot(h, wu_ref[...])
    act = (gate * jax.nn.sigmoid(gate) * up).astype(MXU_DTYPE)
    part = _dot(act, wo_ref[...])

    @pl.when(k == 0)
    def _():
        o_ref[...] = part

    @pl.when(k > 0)
    def _():
        o_ref[...] += part

    @pl.when(k == pl.num_programs(1) - 1)
    def _():
        o_ref[...] = x_ref[...] + gt_ref[...] * o_ref[...]


def _ffn(x2, seq, g, shift, scale, gate, w_in, w_out):
    n, d = x2.shape
    hid = w_out.shape[0]
    tm = _row_tile(n, seq, 512)
    th = 512
    assert hid % th == 0
    nk = hid // th
    per_b = seq // tm
    bsz = n // seq
    bvec = lambda i, k: (i // per_b, 0, 0)
    return pl.pallas_call(
        _ffn_kernel,
        out_shape=jax.ShapeDtypeStruct((n, d), F32),
        grid=(n // tm, nk),
        in_specs=[
            pl.BlockSpec((tm, d), lambda i, k: (i, 0)),
            pl.BlockSpec((1, d), lambda i, k: (0, 0)),
            pl.BlockSpec((None, 1, d), bvec),
            pl.BlockSpec((None, 1, d), bvec),
            pl.BlockSpec((None, 1, d), bvec),
            pl.BlockSpec((d, th), lambda i, k: (0, k)),
            pl.BlockSpec((d, th), lambda i, k: (0, k + nk)),
            pl.BlockSpec((th, d), lambda i, k: (k, 0)),
        ],
        out_specs=pl.BlockSpec((tm, d), lambda i, k: (i, 0)),
        scratch_shapes=[pltpu.VMEM((tm, d), MXU_DTYPE)],
        compiler_params=_params(("parallel", "arbitrary")),
        name="ffn",
    )(x2, g.reshape(1, d), shift.reshape(bsz, 1, d), scale.reshape(bsz, 1, d),
      gate.reshape(bsz, 1, d), w_in, w_in, w_out)


def _rms_kernel(x_ref, g_ref, o_ref):
    x = x_ref[...]
    ms = jnp.mean(x * x, axis=-1, keepdims=True)
    o_ref[...] = x * lax.rsqrt(ms + RMS_EPS) * g_ref[...]


def _final_norm(x2, g):
    n, d = x2.shape
    tm = min(512, n)
    return pl.pallas_call(
        _rms_kernel,
        out_shape=jax.ShapeDtypeStruct((n, d), F32),
        grid=(n // tm,),
        in_specs=[pl.BlockSpec((tm, d), lambda i: (i, 0)),
                  pl.BlockSpec((1, d), lambda i: (0, 0))],
        out_specs=pl.BlockSpec((tm, d), lambda i: (i, 0)),
        compiler_params=_params(("parallel",)),
        name="final_norm",
    )(x2, g.reshape(1, d))


def _gla_kernel(q_ref, k_ref, v_ref, g_ref, al_ref, wup_ref, ba_ref, ng_ref, o_ref,
                st_scr, *, q_scale):
    c = GLA_CHUNK

    @pl.when(pl.program_id(2) == 0)
    def _():
        st_scr[...] = jnp.zeros_like(st_scr)

    rows = lax.broadcasted_iota(jnp.int32, (c, V7X_LANES), 0)
    cols = lax.broadcasted_iota(jnp.int32, (c, V7X_LANES), 1)
    causal = rows >= cols
    cols_sub = lax.broadcasted_iota(jnp.int32, (GLA_SUB, V7X_LANES), 1)
    tri = (lax.broadcasted_iota(jnp.int32, (c, c), 0)
           >= lax.broadcasted_iota(jnp.int32, (c, c), 1)).astype(MXU_DTYPE)

    for ci in range(q_ref.shape[0] // c):
        sl = pl.ds(ci * c, c)
        q = q_ref[sl, :] * q_scale
        k = k_ref[sl, :]
        v = v_ref[sl, :].astype(MXU_DTYPE)
        pre = _dot(al_ref[sl, :].astype(MXU_DTYPE), wup_ref[...]) + ba_ref[...]
        la = -_softplus(-pre) * (1.0 / GLA_TAU)
        b = _split_dot_left(tri, la)
        b_last = b[c - 1:c, :]
        st = st_scr[...]
        inter = lax.dot_general((q * jnp.exp(b)).astype(MXU_DTYPE), st.astype(MXU_DTYPE), _NT,
                                preferred_element_type=F32)

        sub = GLA_SUB
        att_rows = []
        for i in range(c // sub):
            lo_r, hi_r = i * sub, (i + 1) * sub
            q_i, b_i = q[lo_r:hi_r], b[lo_r:hi_r]
            if i == 0:
                att_i = jnp.zeros((sub, V7X_LANES), F32)
            else:
                ref = b[lo_r:lo_r + 1]
                q_f = (q_i * jnp.exp(b_i - ref)).astype(MXU_DTYPE)
                k_f = k[:lo_r] * jnp.exp(ref - b[:lo_r])
                k_f = jnp.concatenate([k_f, jnp.zeros((V7X_LANES - lo_r, k_f.shape[1]), F32)],
                                      axis=0).astype(MXU_DTYPE)
                att_i = lax.dot_general(q_f, k_f, _NT, preferred_element_type=F32)
            for j in range(sub):
                s_idx = lo_r + j
                e = jnp.exp(jnp.minimum(b_i - b[s_idx:s_idx + 1], 0.0))
                col = jnp.sum(q_i * k[s_idx:s_idx + 1] * e, axis=1, keepdims=True)
                att_i = jnp.where(cols_sub == s_idx, col, att_i)
            att_rows.append(att_i)
        att = jnp.where(causal, jnp.concatenate(att_rows, axis=0), 0.0)
        v_pad = jnp.concatenate([v_ref[sl, :], jnp.zeros((V7X_LANES - c, v.shape[1]), F32)],
                                axis=0).astype(MXU_DTYPE)
        o = inter + _dot(att.astype(MXU_DTYPE), v_pad)

        kd = (k * jnp.exp(b_last - b)).astype(MXU_DTYPE)
        st_scr[...] = jnp.exp(b_last) * st + lax.dot_general(v, kd, _TN, preferred_element_type=F32)

        ms = jnp.mean(o * o, axis=-1, keepdims=True)
        on = o * lax.rsqrt(ms + RMS_EPS) * ng_ref[...]
        gg = g_ref[sl, :]
        o_ref[sl, :] = (on * (gg * jax.nn.sigmoid(gg))).astype(o_ref.dtype)


def _split_dot_left(m, x):
    hi = x.astype(MXU_DTYPE)
    lo = (x - hi.astype(F32)).astype(MXU_DTYPE)
    return _dot(m, hi) + _dot(m, lo)


def _gla_scan(proj, alow, w_up, b_alpha, norm_g, bsz, seq):
    dk = w_up.shape[1]
    hk = dk // GLA_HEADS
    dv = proj.shape[-1] // 2 - dk
    hv = dv // GLA_HEADS
    tb = min(256, seq)
    proj3 = proj.reshape(bsz, seq, proj.shape[-1])
    alow3 = alow.reshape(bsz, seq, alow.shape[-1])
    kern = functools.partial(_gla_kernel, q_scale=hk ** -0.5)
    return pl.pallas_call(
        kern,
        out_shape=jax.ShapeDtypeStruct((bsz, seq, dv), MXU_DTYPE),
        grid=(bsz, GLA_HEADS, seq // tb),
        in_specs=[
            pl.BlockSpec((None, tb, hk), lambda b, h, t: (b, t, h)),
            pl.BlockSpec((None, tb, hk), lambda b, h, t: (b, t, GLA_HEADS + h)),
            pl.BlockSpec((None, tb, hv), lambda b, h, t: (b, t, (2 * dk) // hv + h)),
            pl.BlockSpec((None, tb, hv), lambda b, h, t: (b, t, (2 * dk + dv) // hv + h)),
            pl.BlockSpec((None, tb, alow3.shape[-1]), lambda b, h, t: (b, t, 0)),
            pl.BlockSpec((w_up.shape[0], hk), lambda b, h, t: (0, h)),
            pl.BlockSpec((1, hk), lambda b, h, t: (0, h)),
            pl.BlockSpec((1, hv), lambda b, h, t: (0, 0)),
        ],
        out_specs=pl.BlockSpec((None, tb, hv), lambda b, h, t: (b, t, h)),
        scratch_shapes=[pltpu.VMEM((hv, hk), F32)],
        compiler_params=_params(("parallel", "parallel", "arbitrary")),
        name="gla_scan",
    )(proj3, proj3, proj3, proj3, alow3, w_up, b_alpha.reshape(1, dk), norm_g.reshape(1, hv))


def _gla_layer(x2, bsz, seq, norm_g_mix, shift, scale, gate, w_in, w_alpha_up, b_alpha, norm_g, w_out):
    d = x2.shape[1]
    dk = w_alpha_up.shape[1]
    main = 2 * dk + 2 * d
    w_main = w_in[:, :main].astype(MXU_DTYPE)
    w_low = jnp.pad(w_in[:, main:], ((0, 0), (0, V7X_LANES - GLA_LOWRANK))).astype(MXU_DTYPE)
    w_up = jnp.pad(w_alpha_up, ((0, V7X_LANES - GLA_LOWRANK), (0, 0))).astype(MXU_DTYPE)
    proj = _nm_matmul(x2, seq, norm_g_mix, shift, scale, w_main, F32, 512, "gla_proj")
    alow = _nm_matmul(x2, seq, norm_g_mix, shift, scale, w_low, F32, V7X_LANES, "gla_lowrank")
    og = _gla_scan(proj, alow, w_up, b_alpha, norm_g, bsz, seq)
    return _res_matmul(og.reshape(bsz * seq, d), w_out.astype(MXU_DTYPE), x2, seq, gate, "gla_out")


def _sb_kernel(q_ref, k_ref, v_ref, o_ref, acc_scr, run_scr, *, scale):
    tq, tk = SB_QTILE, SB_KTILE
    hq = tq // SB_ROW_SPLIT
    qi = pl.program_id(2)
    kr = lax.broadcasted_iota(jnp.int32, (tk, tk), 0)
    kc = lax.broadcasted_iota(jnp.int32, (tk, tk), 1)
    later_m = (kr > kc).astype(MXU_DTYPE)
    later_m2 = jnp.concatenate([later_m, later_m], axis=0)
    q_pos = qi * tq + lax.broadcasted_iota(jnp.int32, (hq, tk), 0)
    k_off = lax.broadcasted_iota(jnp.int32, (hq, tk), 1)

    def tile(kj, masked):
        start = pl.multiple_of(kj * tk, tk)
        ks = k_ref[pl.ds(start, tk), :]
        vs = v_ref[pl.ds(start, tk), :]
        zs = [lax.dot_general(q_ref[h * hq:(h + 1) * hq, :], ks, _NT, preferred_element_type=F32)
              * (scale * LOG2_E) for h in range(SB_ROW_SPLIT)]
        sps = []
        for h, z in enumerate(zs):
            sp = jnp.maximum(z, 0.0) + jnp.log2(1.0 + jnp.exp2(-jnp.abs(z)))
            if masked:
                sp = jnp.where(start + k_off < q_pos + h * hq, sp, 0.0)
            sps.append(sp)
        laters = []
        for sp in sps:
            hi = sp.astype(MXU_DTYPE)
            lo = (sp - hi.astype(F32)).astype(MXU_DTYPE)
            laters.append(_dot(jnp.concatenate([hi, lo], axis=1), later_m2))
        for h in range(SB_ROW_SPLIT):
            rs = slice(h * hq, (h + 1) * hq)
            run = run_scr[rs, :]
            w = jnp.exp2(zs[h] - sps[h] - laters[h] - jnp.concatenate([run] * (tk // V7X_LANES), axis=1))
            if masked:
                w = jnp.where(start + k_off < q_pos + h * hq, w, 0.0)
            acc_scr[rs, :] += _dot(w.astype(MXU_DTYPE), vs)
            run_scr[rs, :] = run + laters[h][:, 0:1] + sps[h][:, 0:1]

    acc_scr[...] = jnp.zeros_like(acc_scr)
    run_scr[...] = jnp.zeros_like(run_scr)
    per_q = tq // tk
    for m in range(per_q):
        tile(qi * per_q + (per_q - 1 - m), True)

    def body(m, carry):
        tile(qi * per_q - 1 - m, False)
        return carry

    lax.fori_loop(0, qi * per_q, body, 0)
    o_ref[...] = acc_scr[...].astype(o_ref.dtype)


def _sb_attention(qkv, bsz, seq):
    d = qkv.shape[-1] // 3
    dh = d // SB_HEADS
    assert seq % SB_QTILE == 0 and SB_QTILE % SB_KTILE == 0
    qkv3 = qkv.reshape(bsz, seq, 3 * d)
    kern = functools.partial(_sb_kernel, scale=dh ** -0.5)
    return pl.pallas_call(
        kern,
        out_shape=jax.ShapeDtypeStruct((bsz, seq, d), MXU_DTYPE),
        grid=(bsz, SB_HEADS, seq // SB_QTILE),
        in_specs=[
            pl.BlockSpec((None, SB_QTILE, dh), lambda b, h, i: (b, i, h)),
            pl.BlockSpec((None, seq, dh), lambda b, h, i: (b, 0, SB_HEADS + h)),
            pl.BlockSpec((None, seq, dh), lambda b, h, i: (b, 0, 2 * SB_HEADS + h)),
        ],
        out_specs=pl.BlockSpec((None, SB_QTILE, dh), lambda b, h, i: (b, i, h)),
        scratch_shapes=[pltpu.VMEM((SB_QTILE, dh), F32), pltpu.VMEM((SB_QTILE, V7X_LANES), F32)],
        compiler_params=_params(("parallel", "parallel", "arbitrary")),
        name="sb_attention",
    )(qkv3, qkv3, qkv3)


def _sb_layer(x2, bsz, seq, norm_g_mix, shift, scale, gate, w_in, w_out):
    d = x2.shape[1]
    qkv = _nm_matmul(x2, seq, norm_g_mix, shift, scale, w_in.astype(MXU_DTYPE), MXU_DTYPE, 512, "sb_proj")
    o = _sb_attention(qkv, bsz, seq)
    return _res_matmul(o.reshape(bsz * seq, d), w_out.astype(MXU_DTYPE), x2, seq, gate, "sb_out")


def _rw_proj_kernel(mix_ids_ref, x_ref, xp_ref, g_ref, sh_ref, sc_ref, mu_ref, w_ref, o_ref, mix_scr,
                    *, tiles_per_seq):
    i = pl.program_id(0)
    j = pl.program_id(1)

    @pl.when(j == 0)
    def _():
        g = g_ref[...]
        sh = sh_ref[...]
        sc = sc_ref[...]
        h = _norm_mod(x_ref[...], g, sh, sc)
        hp = _norm_mod(xp_ref[...], g, sh, sc)[7:8, :]
        hp = jnp.where(i % tiles_per_seq == 0, 0.0, hp)
        rows = lax.broadcasted_iota(jnp.int32, h.shape, 0)
        prev = jnp.where(rows == 0, hp, pltpu.roll(h, 1, 0))
        xx = prev - h
        for n in range(mix_scr.shape[0]):
            mix_scr[n] = (h + xx * mu_ref[n:n + 1, :]).astype(mix_scr.dtype)

    o_ref[...] = _dot(mix_scr[mix_ids_ref[j]], w_ref[...])


def _rw_proj(x2, seq, g, shift, scale, mu, w_all, mix_ids, tn):
    n, d = x2.shape
    n_out = w_all.shape[1]
    tm = _row_tile(n, seq, 512)
    per_b = seq // tm
    bsz = n // seq
    kern = functools.partial(_rw_proj_kernel, tiles_per_seq=per_b)
    bvec = lambda i, j, ids: (i // per_b, 0, 0)
    return pl.pallas_call(
        kern,
        out_shape=jax.ShapeDtypeStruct((n, n_out), F32),
        grid_spec=pltpu.PrefetchScalarGridSpec(
            num_scalar_prefetch=1,
            grid=(n // tm, n_out // tn),
            in_specs=[
                pl.BlockSpec((tm, d), lambda i, j, ids: (i, 0)),
                pl.BlockSpec((8, d), lambda i, j, ids: (jnp.maximum(i * (tm // 8) - 1, 0), 0)),
                pl.BlockSpec((1, d), lambda i, j, ids: (0, 0)),
                pl.BlockSpec((None, 1, d), bvec),
                pl.BlockSpec((None, 1, d), bvec),
                pl.BlockSpec((8, d), lambda i, j, ids: (0, 0)),
                pl.BlockSpec((d, tn), lambda i, j, ids: (0, j)),
            ],
            out_specs=pl.BlockSpec((tm, tn), lambda i, j, ids: (i, j)),
            scratch_shapes=[pltpu.VMEM((6, tm, d), MXU_DTYPE)],
        ),
        compiler_params=_params(("parallel", "arbitrary")),
        name="rwkv_proj",
    )(mix_ids, x2, x2, g.reshape(1, d), shift.reshape(bsz, 1, d), scale.reshape(bsz, 1, d),
      jnp.pad(mu, ((0, 2), (0, 0))), w_all)


def _lora_up_kernel(p_ref, w_ref, o_ref, *, act):
    p = p_ref[...]
    if act == "tanh":
        p = jnp.tanh(p)
    elif act == "sigmoid":
        p = jax.nn.sigmoid(p)
    o_ref[...] = _dot(p.astype(MXU_DTYPE), w_ref[...])


def _lora_up(proj, col_block, w2, act, tn_in, name):
    n = proj.shape[0]
    d = w2.shape[1]
    tm = min(512, n)
    kern = functools.partial(_lora_up_kernel, act=act)
    return pl.pallas_call(
        kern,
        out_shape=jax.ShapeDtypeStruct((n, d), F32),
        grid=(n // tm,),
        in_specs=[pl.BlockSpec((tm, tn_in), lambda i: (i, col_block)),
                  pl.BlockSpec((tn_in, d), lambda i: (0, 0))],
        out_specs=pl.BlockSpec((tm, d), lambda i: (i, 0)),
        compiler_params=_params(("parallel",)),
        name=name,
    )(proj, w2)


def _rw_chunk_kernel(r_ref, k_ref, v_ref, wp_ref, ap_ref, gt_ref,
                     w0_ref, a0_ref, kk_ref, ka_ref, rk_ref, gng_ref, gnb_ref, o_ref,
                     zt_scr, ops_scr, y_scr):
    hd = RW_HEAD_DIM
    cs = RW_CHUNK
    qw = RW_QUAD * hd
    tb, d = r_ref.shape
    n_quads = d // qw
    n_pairs = d // V7X_LANES

    @pl.when(pl.program_id(1) == 0)
    def _():
        zt_scr[...] = jnp.zeros_like(zt_scr)

    def iota(shape, dim):
        return lax.broadcasted_iota(jnp.int32, shape, dim)

    def shr(x, n):
        return lax.shift_right_logical(x, n)

    lg_hd, lg_cs = hd.bit_length() - 1, cs.bit_length() - 1
    li, lj = shr(iota((V7X_LANES, V7X_LANES), 0), lg_hd), shr(iota((V7X_LANES, V7X_LANES), 1), lg_hd)
    pair_ones = (li == lj).astype(MXU_DTYPE)

    def head_sum(x):
        return jnp.concatenate(
            [_split_dot(x[:, c * V7X_LANES:(c + 1) * V7X_LANES], pair_ones) for c in range(n_pairs)], axis=1)

    r = r_ref[...]
    k = k_ref[...]
    v = v_ref[...]
    w_log = -_softplus(-(w0_ref[...] + wp_ref[...])) - 0.5
    a = jax.nn.sigmoid(a0_ref[...] + ap_ref[...])
    kk = k * kk_ref[...]
    kk = kk / jnp.maximum(jnp.sqrt(head_sum(kk * kk)), 1e-12)
    k2 = k * (1.0 + (a - 1.0) * ka_ref[...])
    for n, val in enumerate((-jnp.exp(w_log), -kk, kk * a, k2, r, v)):
        ops_scr[n] = val

    nb = RW_QUAD * cs
    tri = (iota((cs, cs), 0) >= iota((cs, cs), 1)).astype(MXU_DTYPE)
    same_head_rows = shr(iota((nb, qw), 0), lg_cs) == shr(iota((nb, qw), 1), lg_hd)
    col = iota((cs, V7X_LANES), 1)
    row = iota((cs, V7X_LANES), 0)
    col_s = lax.bitwise_and(col, cs - 1)
    m_ak = (col >= nb) & (col_s < row)
    m_n = col_s <= row
    lr, lc = iota((nb, nb), 0), iota((nb, nb), 1)
    m_lbd = (shr(lr, lg_cs) == shr(lc, lg_cs)) & (lax.bitwise_and(lc, cs - 1) < lax.bitwise_and(lr, cs - 1))
    eye = (lr == lc).astype(F32)
    m_state = shr(iota((qw, qw), 0), lg_hd) == shr(iota((qw, qw), 1), lg_hd)

    def mm(x, y, dims=(((1,), (0,)), ((), ()))):
        return lax.dot_general(x.astype(MXU_DTYPE), y.astype(MXU_DTYPE), dims, preferred_element_type=F32)

    def stack4(x):
        return jnp.where(same_head_rows, jnp.concatenate([x] * RW_QUAD, axis=0), 0.0)

    def chunk(c, carry):
        sl = pl.ds(pl.multiple_of(c * cs, cs), cs)
        lw = ops_scr[0, sl, :]
        cl = _split_dot_left(tri, lw)
        g_in = jnp.exp(cl)
        g_inv = jnp.exp(-cl)
        at_all = ops_scr[1, sl, :] * jnp.exp(cl - lw)
        bt_all = ops_scr[2, sl, :] * g_inv
        kt_all = ops_scr[3, sl, :] * g_inv
        rt_all = ops_scr[4, sl, :] * g_in
        v_all = ops_scr[5, sl, :]
        g_last = g_in[cs - 1:cs, :]
        quads = range(n_quads)
        qs = lambda x, q: x[:, q * qw:(q + 1) * qw]

        lhs = [jnp.concatenate([qs(at_all, q), qs(rt_all, q)], axis=0) for q in quads]
        vbd = [stack4(qs(v_all, q)) for q in quads]
        gram = [mm(lhs[q], jnp.concatenate([stack4(qs(bt_all, q)), stack4(qs(kt_all, q))], axis=0), _NT)
                for q in quads]
        l1 = [jnp.where(m_lbd, jnp.concatenate([gram[q][:cs, :nb]] * RW_QUAD, axis=0), 0.0)
              for q in quads]
        lak_v = [mm(jnp.where(m_ak, gram[q][:cs], 0.0), jnp.concatenate([vbd[q], vbd[q]], axis=0))
                 for q in quads]
        inv = [eye + l1[q] for q in quads]
        pw = l1
        for _ in range(lg_cs - 1):
            pw = [mm(pw[q], pw[q]) for q in quads]
            inv = [inv[q] + mm(inv[q], pw[q]) for q in quads]

        az = [mm(lhs[q], zt_scr[q], _NT) for q in quads]
        ubig = [mm(inv[q], stack4(az[q][:cs] + lak_v[q])) for q in quads]
        u = [sum(ubig[q][h * cs:(h + 1) * cs] for h in range(RW_QUAD)) for q in quads]
        y = [az[q][cs:] + mm(jnp.where(m_n, gram[q][cs:], 0.0),
                             jnp.concatenate([stack4(u[q]), vbd[q]], axis=0)) for q in quads]
        for q in quads:
            upd = mm(jnp.concatenate([u[q], qs(v_all, q)], axis=0),
                     jnp.concatenate([qs(bt_all, q), qs(kt_all, q)], axis=0), _TN)
            zt_scr[q] = (zt_scr[q] + jnp.where(m_state, upd, 0.0)) * qs(g_last, q)
            y_scr[sl, q * qw:(q + 1) * qw] = y[q]
        return carry

    lax.fori_loop(0, tb // cs, chunk, 0)

    y = y_scr[...]
    inv_hd = 1.0 / hd
    mean = head_sum(y) * inv_hd
    dlt = y - mean
    var = head_sum(dlt * dlt) * inv_hd
    yn = dlt * lax.rsqrt(var + RW_GN_EPS) * gng_ref[...] + gnb_ref[...]
    bonus = head_sum(r * k2 * rk_ref[...]) * v
    o_ref[...] = ((yn + bonus) * gt_ref[...]).astype(o_ref.dtype)


def _rw_chunk_scan(proj, w_pre, a_pre, g_out, w0, a0, k_k, k_a, r_k, gn_g, gn_b, bsz, seq, d):
    qw = RW_QUAD * RW_HEAD_DIM
    assert seq % RW_TBLK == 0 and RW_TBLK % RW_CHUNK == 0 and d % qw == 0
    assert 2 * RW_QUAD * RW_CHUNK == V7X_LANES
    n_quads = d // qw
    p3 = proj.reshape(bsz, seq, proj.shape[-1])
    act = lambda off: pl.BlockSpec((None, RW_TBLK, d), lambda b, t: (b, t, off))
    vec = pl.BlockSpec((1, d), lambda b, t: (0, 0))
    as3 = lambda z: z.reshape(bsz, seq, d)
    as_row = lambda z: z.reshape(1, d)
    return pl.pallas_call(
        _rw_chunk_kernel,
        out_shape=jax.ShapeDtypeStruct((bsz, seq, d), MXU_DTYPE),
        grid=(bsz, seq // RW_TBLK),
        in_specs=[act(0), act(1), act(2), act(0), act(0), act(0)] + [vec] * 7,
        out_specs=pl.BlockSpec((None, RW_TBLK, d), lambda b, t: (b, t, 0)),
        scratch_shapes=[
            pltpu.VMEM((n_quads, qw, qw), F32),
            pltpu.VMEM((6, RW_TBLK, d), F32),
            pltpu.VMEM((RW_TBLK, d), F32),
        ],
        compiler_params=_params(("parallel", "arbitrary")),
        name="rwkv_scan",
    )(p3, p3, p3, as3(w_pre), as3(a_pre), as3(g_out),
      as_row(w0), as_row(a0), as_row(k_k), as_row(k_a), as_row(r_k), as_row(gn_g), as_row(gn_b))


def _rw_layer(x2, bsz, seq, norm_g_mix, shift, scale, gate, mu, w_rkv, w0, w1, w2, a0, a1, a2, g1, g2,
              k_k, k_a, r_k, gn_g, gn_b, w_out):
    d = x2.shape[1]
    tn = 512
    pad_c = lambda m: jnp.pad(m, ((0, 0), (0, tn - m.shape[1])))
    pad_r = lambda m: jnp.pad(m, ((0, tn - m.shape[0]), (0, 0))).astype(MXU_DTYPE)
    w_all = jnp.concatenate([w_rkv[0], w_rkv[1], w_rkv[2], pad_c(w1), pad_c(a1), pad_c(g1)],
                            axis=1).astype(MXU_DTYPE)
    per = d // tn
    mix_ids = jnp.array([0] * per + [2] * per + [3] * per + [1, 4, 5], jnp.int32)
    proj = _rw_proj(x2, seq, norm_g_mix, shift, scale, mu, w_all, mix_ids, tn)
    w_pre = _lora_up(proj, 3 * per, pad_r(w2), "tanh", tn, "rwkv_decay_lora")
    a_pre = _lora_up(proj, 3 * per + 1, pad_r(a2), "none", tn, "rwkv_a_lora")
    g_out = _lora_up(proj, 3 * per + 2, pad_r(g2), "sigmoid", tn, "rwkv_gate_lora")
    yg = _rw_chunk_scan(proj, w_pre, a_pre, g_out, w0, a0, k_k, k_a, r_k, gn_g, gn_b, bsz, seq, d)
    return _res_matmul(yg.reshape(bsz * seq, d), w_out.astype(MXU_DTYPE), x2, seq, gate, "rwkv_out")


def kernel(x, c, norm_mix_g, norm_ffn_g, w_mod, b_mod, ffn_w_in, ffn_w_out, final_g, gla_w_in, gla_w_alpha_up, gla_b_alpha, gla_norm_g, gla_w_out, sb_w_in, sb_w_out, rw_mu, rw_w_rkv, rw_w0, rw_w1, rw_w2, rw_a0, rw_a1, rw_a2, rw_g1, rw_g2, rw_k_k, rw_k_a, rw_r_k, rw_gn_g, rw_gn_b, rw_w_out):
    bsz, seq, d = x.shape
    depth = w_mod.shape[0]
    mod = _mod_all(c, w_mod, b_mod)
    x2 = x.reshape(bsz * seq, d)
    for i in range(depth):
        sh_m, sc_m, gt_m, sh_f, sc_f, gt_f = (mod[i, :, n * d:(n + 1) * d] for n in range(6))
        kind, j = i % 3, i // 3
        if kind == 0:
            x2 = _gla_layer(x2, bsz, seq, norm_mix_g[i], sh_m, sc_m, gt_m, gla_w_in[j],
                            gla_w_alpha_up[j], gla_b_alpha[j], gla_norm_g[j], gla_w_out[j])
        elif kind == 1:
            x2 = _sb_layer(x2, bsz, seq, norm_mix_g[i], sh_m, sc_m, gt_m, sb_w_in[j], sb_w_out[j])
        else:
            x2 = _rw_layer(x2, bsz, seq, norm_mix_g[i], sh_m, sc_m, gt_m, rw_mu[j], rw_w_rkv[j],
                           rw_w0[j], rw_w1[j], rw_w2[j], rw_a0[j], rw_a1[j], rw_a2[j], rw_g1[j],
                           rw_g2[j], rw_k_k[j], rw_k_a[j], rw_r_k[j], rw_gn_g[j], rw_gn_b[j],
                           rw_w_out[j])
        x2 = _ffn(x2, seq, norm_ffn_g[i], sh_f, sc_f, gt_f,
                  ffn_w_in[i].astype(MXU_DTYPE), ffn_w_out[i].astype(MXU_DTYPE))
    return _final_norm(x2, final_g).reshape(bsz, seq, d)
```

```python
import functools

import jax
import jax.numpy as jnp
from jax import lax
from jax.experimental import pallas as pl
from jax.experimental.pallas import tpu as pltpu

F32 = jnp.float32
MXU_DTYPE = jnp.bfloat16

V7X_LANES = 128
V7X_VMEM_LIMIT_BYTES = 56 * 1024 * 1024

RMS_EPS = 1e-6
LOG2_E = 1.4426950408889634
GLA_HEADS = 4
GLA_LOWRANK = 16
GLA_TAU = 16.0
GLA_CHUNK = 64
GLA_SUB = 16
SB_HEADS = 16
SB_QTILE = 512
SB_KTILE = 256
SB_ROW_SPLIT = 2
RW_HEAD_DIM = 64
RW_GN_EPS = 64e-5
RW_TBLK = 128
RW_CHUNK = 16
RW_QUAD = 4

_NT = (((1,), (1,)), ((), ()))
_TN = (((0,), (0,)), ((), ()))


def _params(semantics):
    return pltpu.CompilerParams(dimension_semantics=semantics,
                                vmem_limit_bytes=V7X_VMEM_LIMIT_BYTES)


def _row_tile(n_rows, seq, want):
    t = min(want, seq)
    assert seq % t == 0 and n_rows % t == 0
    return t


def _dot(a, b):
    return jnp.dot(a, b, preferred_element_type=F32)


def _split_dot(x, m):
    hi = x.astype(MXU_DTYPE)
    lo = (x - hi.astype(F32)).astype(MXU_DTYPE)
    return _dot(hi, m) + _dot(lo, m)


def _softplus(z):
    return jnp.maximum(z, 0.0) + jnp.log1p(jnp.exp(-jnp.abs(z)))


def _norm_mod(x, g, shift, scale):
    ms = jnp.mean(x * x, axis=-1, keepdims=True)
    y = x * lax.rsqrt(ms + RMS_EPS) * g
    return y * (1.0 + scale) + shift


def _mod_kernel(c_ref, w_ref, b_ref, o_ref):
    c = c_ref[...]
    ca = (c * jax.nn.sigmoid(c)).astype(MXU_DTYPE)
    o_ref[...] = _dot(ca, w_ref[...].astype(MXU_DTYPE)) + b_ref[...]


def _mod_all(c, w_mod, b_mod):
    depth, d, six_d = w_mod.shape
    bsz = c.shape[0]
    bp = -(-bsz // 8) * 8
    cp = jnp.pad(c, ((0, bp - bsz), (0, 0)))
    tn = 1024
    out = pl.pallas_call(
        _mod_kernel,
        out_shape=jax.ShapeDtypeStruct((depth, bp, six_d), F32),
        grid=(depth, six_d // tn),
        in_specs=[
            pl.BlockSpec((bp, d), lambda l, j: (0, 0)),
            pl.BlockSpec((None, d, tn), lambda l, j: (l, 0, j)),
            pl.BlockSpec((None, 1, tn), lambda l, j: (l, 0, j)),
        ],
        out_specs=pl.BlockSpec((None, bp, tn), lambda l, j: (l, 0, j)),
        compiler_params=_params(("arbitrary", "arbitrary")),
        name="mod_proj",
    )(cp, w_mod, b_mod.reshape(depth, 1, six_d))
    return out[:, :bsz]


def _nm_matmul_kernel(x_ref, g_ref, sh_ref, sc_ref, w_ref, o_ref, h_scr):
    @pl.when(pl.program_id(1) == 0)
    def _():
        h_scr[...] = _norm_mod(x_ref[...], g_ref[...], sh_ref[...], sc_ref[...]).astype(h_scr.dtype)

    o_ref[...] = _dot(h_scr[...], w_ref[...]).astype(o_ref.dtype)


def _nm_matmul(x2, seq, g, shift, scale, w, out_dtype, tn, name):
    n, d = x2.shape
    n_out = w.shape[1]
    tm = _row_tile(n, seq, 512)
    per_b = seq // tm
    bsz = n // seq
    return pl.pallas_call(
        _nm_matmul_kernel,
        out_shape=jax.ShapeDtypeStruct((n, n_out), out_dtype),
        grid=(n // tm, n_out // tn),
        in_specs=[
            pl.BlockSpec((tm, d), lambda i, j: (i, 0)),
            pl.BlockSpec((1, d), lambda i, j: (0, 0)),
            pl.BlockSpec((None, 1, d), lambda i, j: (i // per_b, 0, 0)),
            pl.BlockSpec((None, 1, d), lambda i, j: (i // per_b, 0, 0)),
            pl.BlockSpec((d, tn), lambda i, j: (0, j)),
        ],
        out_specs=pl.BlockSpec((tm, tn), lambda i, j: (i, j)),
        scratch_shapes=[pltpu.VMEM((tm, d), MXU_DTYPE)],
        compiler_params=_params(("parallel", "arbitrary")),
        name=name,
    )(x2, g.reshape(1, d), shift.reshape(bsz, 1, d), scale.reshape(bsz, 1, d), w)


def _res_matmul_kernel(a_ref, w_ref, x_ref, gt_ref, o_ref):
    o_ref[...] = x_ref[...] + gt_ref[...] * _dot(a_ref[...], w_ref[...])


def _res_matmul(a2, w, x2, seq, gate, name):
    n, k = a2.shape
    d = w.shape[1]
    tm = _row_tile(n, seq, 512)
    tn = min(2048, d)
    per_b = seq // tm
    bsz = n // seq
    return pl.pallas_call(
        _res_matmul_kernel,
        out_shape=jax.ShapeDtypeStruct((n, d), F32),
        grid=(n // tm, d // tn),
        in_specs=[
            pl.BlockSpec((tm, k), lambda i, j: (i, 0)),
            pl.BlockSpec((k, tn), lambda i, j: (0, j)),
            pl.BlockSpec((tm, tn), lambda i, j: (i, j)),
            pl.BlockSpec((None, 1, tn), lambda i, j: (i // per_b, 0, j)),
        ],
        out_specs=pl.BlockSpec((tm, tn), lambda i, j: (i, j)),
        compiler_params=_params(("parallel", "arbitrary")),
        name=name,
    )(a2, w, x2, gate.reshape(bsz, 1, d))


def _ffn_kernel(x_ref, g_ref, sh_ref, sc_ref, gt_ref, wg_ref, wu_ref, wo_ref, o_ref, h_scr):
    k = pl.program_id(1)

    @pl.when(k == 0)
    def _():
        h_scr[...] = _norm_mod(x_ref[...], g_ref[...], sh_ref[...], sc_ref[...]).astype(h_scr.dtype)

    h = h_scr[...]
    gate = _dot(h, wg_ref[...])
    up = _dot(h, wu_ref[...])
    act = (gate * jax.nn.sigmoid(gate) * up).astype(MXU_DTYPE)
    part = _dot(act, wo_ref[...])

    @pl.when(k == 0)
    def _():
        o_ref[...] = part

    @pl.when(k > 0)
    def _():
        o_ref[...] += part

    @pl.when(k == pl.num_programs(1) - 1)
    def _():
        o_ref[...] = x_ref[...] + gt_ref[...] * o_ref[...]


def _ffn(x2, seq, g, shift, scale, gate, w_in, w_out):
    n, d = x2.shape
    hid = w_out.shape[0]
    tm = _row_tile(n, seq, 512)
    th = 512
    assert hid % th == 0
    nk = hid // th
    per_b = seq // tm
    bsz = n // seq
    bvec = lambda i, k: (i // per_b, 0, 0)
    return pl.pallas_call(
        _ffn_kernel,
        out_shape=jax.ShapeDtypeStruct((n, d), F32),
        grid=(n // tm, nk),
        in_specs=[
            pl.BlockSpec((tm, d), lambda i, k: (i, 0)),
            pl.BlockSpec((1, d), lambda i, k: (0, 0)),
            pl.BlockSpec((None, 1, d), bvec),
            pl.BlockSpec((None, 1, d), bvec),
            pl.BlockSpec((None, 1, d), bvec),
            pl.BlockSpec((d, th), lambda i, k: (0, k)),
            pl.BlockSpec((d, th), lambda i, k: (0, k + nk)),
            pl.BlockSpec((th, d), lambda i, k: (k, 0)),
        ],
        out_specs=pl.BlockSpec((tm, d), lambda i, k: (i, 0)),
        scratch_shapes=[pltpu.VMEM((tm, d), MXU_DTYPE)],
        compiler_params=_params(("parallel", "arbitrary")),
        name="ffn",
    )(x2, g.reshape(1, d), shift.reshape(bsz, 1, d), scale.reshape(bsz, 1, d),
      gate.reshape(bsz, 1, d), w_in, w_in, w_out)


def _rms_kernel(x_ref, g_ref, o_ref):
    x = x_ref[...]
    ms = jnp.mean(x * x, axis=-1, keepdims=True)
    o_ref[...] = x * lax.rsqrt(ms + RMS_EPS) * g_ref[...]


def _final_norm(x2, g):
    n, d = x2.shape
    tm = min(512, n)
    return pl.pallas_call(
        _rms_kernel,
        out_shape=jax.ShapeDtypeStruct((n, d), F32),
        grid=(n // tm,),
        in_specs=[pl.BlockSpec((tm, d), lambda i: (i, 0)),
                  pl.BlockSpec((1, d), lambda i: (0, 0))],
        out_specs=pl.BlockSpec((tm, d), lambda i: (i, 0)),
        compiler_params=_params(("parallel",)),
        name="final_norm",
    )(x2, g.reshape(1, d))


def _gla_kernel(q_ref, k_ref, v_ref, g_ref, al_ref, wup_ref, ba_ref, ng_ref, o_ref,
                st_scr, *, q_scale):
    c = GLA_CHUNK

    @pl.when(pl.program_id(2) == 0)
    def _():
        st_scr[...] = jnp.zeros_like(st_scr)

    rows = lax.broadcasted_iota(jnp.int32, (c, V7X_LANES), 0)
    cols = lax.broadcasted_iota(jnp.int32, (c, V7X_LANES), 1)
    causal = rows >= cols
    cols_sub = lax.broadcasted_iota(jnp.int32, (GLA_SUB, V7X_LANES), 1)
    tri = (lax.broadcasted_iota(jnp.int32, (c, c), 0)
           >= lax.broadcasted_iota(jnp.int32, (c, c), 1)).astype(MXU_DTYPE)

    for ci in range(q_ref.shape[0] // c):
        sl = pl.ds(ci * c, c)
        q = q_ref[sl, :] * q_scale
        k = k_ref[sl, :]
        v = v_ref[sl, :].astype(MXU_DTYPE)
        pre = _dot(al_ref[sl, :].astype(MXU_DTYPE), wup_ref[...]) + ba_ref[...]
        la = -_softplus(-pre) * (1.0 / GLA_TAU)
        b = _split_dot_left(tri, la)
        b_last = b[c - 1:c, :]
        st = st_scr[...]
        inter = lax.dot_general((q * jnp.exp(b)).astype(MXU_DTYPE), st.astype(MXU_DTYPE), _NT,
                                preferred_element_type=F32)

        sub = GLA_SUB
        att_rows = []
        for i in range(c // sub):
            lo_r, hi_r = i * sub, (i + 1) * sub
            q_i, b_i = q[lo_r:hi_r], b[lo_r:hi_r]
            if i == 0:
                att_i = jnp.zeros((sub, V7X_LANES), F32)
            else:
                ref = b[lo_r:lo_r + 1]
                q_f = (q_i * jnp.exp(b_i - ref)).astype(MXU_DTYPE)
                k_f = k[:lo_r] * jnp.exp(ref - b[:lo_r])
                k_f = jnp.concatenate([k_f, jnp.zeros((V7X_LANES - lo_r, k_f.shape[1]), F32)],
                                      axis=0).astype(MXU_DTYPE)
                att_i = lax.dot_general(q_f, k_f, _NT, preferred_element_type=F32)
            for j in range(sub):
                s_idx = lo_r + j
                e = jnp.exp(jnp.minimum(b_i - b[s_idx:s_idx + 1], 0.0))
                col = jnp.sum(q_i * k[s_idx:s_idx + 1] * e, axis=1, keepdims=True)
                att_i = jnp.where(cols_sub == s_idx, col, att_i)
            att_rows.append(att_i)
        att = jnp.where(causal, jnp.concatenate(att_rows, axis=0), 0.0)
        v_pad = jnp.concatenate([v_ref[sl, :], jnp.zeros((V7X_LANES - c, v.shape[1]), F32)],
                                axis=0).astype(MXU_DTYPE)
        o = inter + _dot(att.astype(MXU_DTYPE), v_pad)

        kd = (k * jnp.exp(b_last - b)).astype(MXU_DTYPE)
        st_scr[...] = jnp.exp(b_last) * st + lax.dot_general(v, kd, _TN, preferred_element_type=F32)

        ms = jnp.mean(o * o, axis=-1, keepdims=True)
        on = o * lax.rsqrt(ms + RMS_EPS) * ng_ref[...]
        gg = g_ref[sl, :]
        o_ref[sl, :] = (on * (gg * jax.nn.sigmoid(gg))).astype(o_ref.dtype)


def _split_dot_left(m, x):
    hi = x.astype(MXU_DTYPE)
    lo = (x - hi.astype(F32)).astype(MXU_DTYPE)
    return _dot(m, hi) + _dot(m, lo)


def _gla_scan(proj, alow, w_up, b_alpha, norm_g, bsz, seq):
    dk = w_up.shape[1]
    hk = dk // GLA_HEADS
    dv = proj.shape[-1] // 2 - dk
    hv = dv // GLA_HEADS
    tb = min(256, seq)
    proj3 = proj.reshape(bsz, seq, proj.shape[-1])
    alow3 = alow.reshape(bsz, seq, alow.shape[-1])
    kern = functools.partial(_gla_kernel, q_scale=hk ** -0.5)
    return pl.pallas_call(
        kern,
        out_shape=jax.ShapeDtypeStruct((bsz, seq, dv), MXU_DTYPE),
        grid=(bsz, GLA_HEADS, seq // tb),
        in_specs=[
            pl.BlockSpec((None, tb, hk), lambda b, h, t: (b, t, h)),
            pl.BlockSpec((None, tb, hk), lambda b, h, t: (b, t, GLA_HEADS + h)),
            pl.BlockSpec((None, tb, hv), lambda b, h, t: (b, t, (2 * dk) // hv + h)),
            pl.BlockSpec((None, tb, hv), lambda b, h, t: (b, t, (2 * dk + dv) // hv + h)),
            pl.BlockSpec((None, tb, alow3.shape[-1]), lambda b, h, t: (b, t, 0)),
            pl.BlockSpec((w_up.shape[0], hk), lambda b, h, t: (0, h)),
            pl.BlockSpec((1, hk), lambda b, h, t: (0, h)),
            pl.BlockSpec((1, hv), lambda b, h, t: (0, 0)),
        ],
        out_specs=pl.BlockSpec((None, tb, hv), lambda b, h, t: (b, t, h)),
        scratch_shapes=[pltpu.VMEM((hv, hk), F32)],
        compiler_params=_params(("parallel", "parallel", "arbitrary")),
        name="gla_scan",
    )(proj3, proj3, proj3, proj3, alow3, w_up, b_alpha.reshape(1, dk), norm_g.reshape(1, hv))


def _gla_layer(x2, bsz, seq, norm_g_mix, shift, scale, gate, w_in, w_alpha_up, b_alpha, norm_g, w_out):
    d = x2.shape[1]
    dk = w_alpha_up.shape[1]
    main = 2 * dk + 2 * d
    w_main = w_in[:, :main].astype(MXU_DTYPE)
    w_low = jnp.pad(w_in[:, main:], ((0, 0), (0, V7X_LANES - GLA_LOWRANK))).astype(MXU_DTYPE)
    w_up = jnp.pad(w_alpha_up, ((0, V7X_LANES - GLA_LOWRANK), (0, 0))).astype(MXU_DTYPE)
    proj = _nm_matmul(x2, seq, norm_g_mix, shift, scale, w_main, F32, 2048, "gla_proj")
    alow = _nm_matmul(x2, seq, norm_g_mix, shift, scale, w_low, F32, V7X_LANES, "gla_lowrank")
    og = _gla_scan(proj, alow, w_up, b_alpha, norm_g, bsz, seq)
    return _res_matmul(og.reshape(bsz * seq, d), w_out.astype(MXU_DTYPE), x2, seq, gate, "gla_out")


def _sb_kernel(q_ref, k_ref, v_ref, o_ref, acc_scr, run_scr, *, scale):
    tq, tk = SB_QTILE, SB_KTILE
    hq = tq // SB_ROW_SPLIT
    qi = pl.program_id(2)
    kr = lax.broadcasted_iota(jnp.int32, (tk, tk), 0)
    kc = lax.broadcasted_iota(jnp.int32, (tk, tk), 1)
    later_m = (kr > kc).astype(MXU_DTYPE)
    later_m2 = jnp.concatenate([later_m, later_m], axis=0)
    q_pos = qi * tq + lax.broadcasted_iota(jnp.int32, (hq, tk), 0)
    k_off = lax.broadcasted_iota(jnp.int32, (hq, tk), 1)

    def tile(kj, masked):
        start = pl.multiple_of(kj * tk, tk)
        ks = k_ref[pl.ds(start, tk), :]
        vs = v_ref[pl.ds(start, tk), :]
        zs = [lax.dot_general(q_ref[h * hq:(h + 1) * hq, :], ks, _NT, preferred_element_type=F32)
              * (scale * LOG2_E) for h in range(SB_ROW_SPLIT)]
        sps = []
        for h, z in enumerate(zs):
            sp = jnp.maximum(z, 0.0) + jnp.log2(1.0 + jnp.exp2(-jnp.abs(z)))
            if masked:
                sp = jnp.where(start + k_off < q_pos + h * hq, sp, 0.0)
            sps.append(sp)
        laters = []
        for sp in sps:
            hi = sp.astype(MXU_DTYPE)
            lo = (sp - hi.astype(F32)).astype(MXU_DTYPE)
            laters.append(_dot(jnp.concatenate([hi, lo], axis=1), later_m2))
        for h in range(SB_ROW_SPLIT):
            rs = slice(h * hq, (h + 1) * hq)
            run = run_scr[rs, :]
            w = jnp.exp2(zs[h] - sps[h] - laters[h] - jnp.concatenate([run] * (tk // V7X_LANES), axis=1))
            if masked:
                w = jnp.where(start + k_off < q_pos + h * hq, w, 0.0)
            acc_scr[rs, :] += _dot(w.astype(MXU_DTYPE), vs)
            run_scr[rs, :] = run + laters[h][:, 0:1] + sps[h][:, 0:1]

    acc_scr[...] = jnp.zeros_like(acc_scr)
    run_scr[...] = jnp.zeros_like(run_scr)
    per_q = tq // tk
    for m in range(per_q):
        tile(qi * per_q + (per_q - 1 - m), True)

    def body(m, carry):
        tile(qi * per_q - 1 - m, False)
        return carry

    lax.fori_loop(0, qi * per_q, body, 0)
    o_ref[...] = acc_scr[...].astype(o_ref.dtype)


def _sb_attention(qkv, bsz, seq):
    d = qkv.shape[-1] // 3
    dh = d // SB_HEADS
    assert seq % SB_QTILE == 0 and SB_QTILE % SB_KTILE == 0
    qkv3 = qkv.reshape(bsz, seq, 3 * d)
    kern = functools.partial(_sb_kernel, scale=dh ** -0.5)
    return pl.pallas_call(
        kern,
        out_shape=jax.ShapeDtypeStruct((bsz, seq, d), MXU_DTYPE),
        grid=(bsz, SB_HEADS, seq // SB_QTILE),
        in_specs=[
            pl.BlockSpec((None, SB_QTILE, dh), lambda b, h, i: (b, i, h)),
            pl.BlockSpec((None, seq, dh), lambda b, h, i: (b, 0, SB_HEADS + h)),
            pl.BlockSpec((None, seq, dh), lambda b, h, i: (b, 0, 2 * SB_HEADS + h)),
        ],
        out_specs=pl.BlockSpec((None, SB_QTILE, dh), lambda b, h, i: (b, i, h)),
        scratch_shapes=[pltpu.VMEM((SB_QTILE, dh), F32), pltpu.VMEM((SB_QTILE, V7X_LANES), F32)],
        compiler_params=_params(("parallel", "parallel", "arbitrary")),
        name="sb_attention",
    )(qkv3, qkv3, qkv3)


def _sb_layer(x2, bsz, seq, norm_g_mix, shift, scale, gate, w_in, w_out):
    d = x2.shape[1]
    qkv = _nm_matmul(x2, seq, norm_g_mix, shift, scale, w_in.astype(MXU_DTYPE), MXU_DTYPE, 2048, "sb_proj")
    o = _sb_attention(qkv, bsz, seq)
    return _res_matmul(o.reshape(bsz * seq, d), w_out.astype(MXU_DTYPE), x2, seq, gate, "sb_out")


def _rw_proj_kernel(mix_ids_ref, x_ref, xp_ref, g_ref, sh_ref, sc_ref, mu_ref, w_ref, o_ref, mix_scr,
                    *, tiles_per_seq):
    i = pl.program_id(0)
    j = pl.program_id(1)

    @pl.when(j == 0)
    def _():
        g = g_ref[...]
        sh = sh_ref[...]
        sc = sc_ref[...]
        h = _norm_mod(x_ref[...], g, sh, sc)
        hp = _norm_mod(xp_ref[...], g, sh, sc)[7:8, :]
        hp = jnp.where(i % tiles_per_seq == 0, 0.0, hp)
        rows = lax.broadcasted_iota(jnp.int32, h.shape, 0)
        prev = jnp.where(rows == 0, hp, pltpu.roll(h, 1, 0))
        xx = prev - h
        for n in range(mix_scr.shape[0]):
            mix_scr[n] = (h + xx * mu_ref[n:n + 1, :]).astype(mix_scr.dtype)

    o_ref[...] = _dot(mix_scr[mix_ids_ref[j]], w_ref[...])


def _rw_proj(x2, seq, g, shift, scale, mu, w_all, mix_ids, tn):
    n, d = x2.shape
    n_out = w_all.shape[1]
    tm = _row_tile(n, seq, 512)
    per_b = seq // tm
    bsz = n // seq
    kern = functools.partial(_rw_proj_kernel, tiles_per_seq=per_b)
    bvec = lambda i, j, ids: (i // per_b, 0, 0)
    return pl.pallas_call(
        kern,
        out_shape=jax.ShapeDtypeStruct((n, n_out), F32),
        grid_spec=pltpu.PrefetchScalarGridSpec(
            num_scalar_prefetch=1,
            grid=(n // tm, n_out // tn),
            in_specs=[
                pl.BlockSpec((tm, d), lambda i, j, ids: (i, 0)),
                pl.BlockSpec((8, d), lambda i, j, ids: (jnp.maximum(i * (tm // 8) - 1, 0), 0)),
                pl.BlockSpec((1, d), lambda i, j, ids: (0, 0)),
                pl.BlockSpec((None, 1, d), bvec),
                pl.BlockSpec((None, 1, d), bvec),
                pl.BlockSpec((8, d), lambda i, j, ids: (0, 0)),
                pl.BlockSpec((d, tn), lambda i, j, ids: (0, j)),
            ],
            out_specs=pl.BlockSpec((tm, tn), lambda i, j, ids: (i, j)),
            scratch_shapes=[pltpu.VMEM((6, tm, d), MXU_DTYPE)],
        ),
        compiler_params=_params(("parallel", "arbitrary")),
        name="rwkv_proj",
    )(mix_ids, x2, x2, g.reshape(1, d), shift.reshape(bsz, 1, d), scale.reshape(bsz, 1, d),
      jnp.pad(mu, ((0, 2), (0, 0))), w_all)


def _lora_up_kernel(p_ref, w_ref, o_ref, *, act):
    p = p_ref[...]
    if act == "tanh":
        p = jnp.tanh(p)
    elif act == "sigmoid":
        p = jax.nn.sigmoid(p)
    o_ref[...] = _dot(p.astype(MXU_DTYPE), w_ref[...])


def _lora_up(proj, col_block, w2, act, tn_in, name):
    n = proj.shape[0]
    d = w2.shape[1]
    tm = min(512, n)
    kern = functools.partial(_lora_up_kernel, act=act)
    return pl.pallas_call(
        kern,
        out_shape=jax.ShapeDtypeStruct((n, d), F32),
        grid=(n // tm,),
        in_specs=[pl.BlockSpec((tm, tn_in), lambda i: (i, col_block)),
                  pl.BlockSpec((tn_in, d), lambda i: (0, 0))],
        out_specs=pl.BlockSpec((tm, d), lambda i: (i, 0)),
        compiler_params=_params(("parallel",)),
        name=name,
    )(proj, w2)


def _rw_chunk_kernel(r_ref, k_ref, v_ref, wp_ref, ap_ref, gt_ref,
                     w0_ref, a0_ref, kk_ref, ka_ref, rk_ref, gng_ref, gnb_ref, o_ref,
                     zt_scr, ops_scr, y_scr):
    hd = RW_HEAD_DIM
    cs = RW_CHUNK
    qw = RW_QUAD * hd
    tb, d = r_ref.shape
    n_quads = d // qw
    n_pairs = d // V7X_LANES

    @pl.when(pl.program_id(1) == 0)
    def _():
        zt_scr[...] = jnp.zeros_like(zt_scr)

    def iota(shape, dim):
        return lax.broadcasted_iota(jnp.int32, shape, dim)

    def shr(x, n):
        return lax.shift_right_logical(x, n)

    lg_hd, lg_cs = hd.bit_length() - 1, cs.bit_length() - 1
    li, lj = shr(iota((V7X_LANES, V7X_LANES), 0), lg_hd), shr(iota((V7X_LANES, V7X_LANES), 1), lg_hd)
    pair_ones = (li == lj).astype(MXU_DTYPE)

    def head_sum(x):
        return jnp.concatenate(
            [_split_dot(x[:, c * V7X_LANES:(c + 1) * V7X_LANES], pair_ones) for c in range(n_pairs)], axis=1)

    r = r_ref[...]
    k = k_ref[...]
    v = v_ref[...]
    w_log = -_softplus(-(w0_ref[...] + wp_ref[...])) - 0.5
    a = jax.nn.sigmoid(a0_ref[...] + ap_ref[...])
    kk = k * kk_ref[...]
    kk = kk / jnp.maximum(jnp.sqrt(head_sum(kk * kk)), 1e-12)
    k2 = k * (1.0 + (a - 1.0) * ka_ref[...])
    for n, val in enumerate((-jnp.exp(w_log), -kk, kk * a, k2, r, v)):
        ops_scr[n] = val

    nb = RW_QUAD * cs
    tri = (iota((cs, cs), 0) >= iota((cs, cs), 1)).astype(MXU_DTYPE)
    same_head_rows = shr(iota((nb, qw), 0), lg_cs) == shr(iota((nb, qw), 1), lg_hd)
    col = iota((cs, V7X_LANES), 1)
    row = iota((cs, V7X_LANES), 0)
    col_s = lax.bitwise_and(col, cs - 1)
    m_ak = (col >= nb) & (col_s < row)
    m_n = col_s <= row
    lr, lc = iota((nb, nb), 0), iota((nb, nb), 1)
    m_lbd = (shr(lr, lg_cs) == shr(lc, lg_cs)) & (lax.bitwise_and(lc, cs - 1) < lax.bitwise_and(lr, cs - 1))
    eye = (lr == lc).astype(F32)
    m_state = shr(iota((qw, qw), 0), lg_hd) == shr(iota((qw, qw), 1), lg_hd)

    def mm(x, y, dims=(((1,), (0,)), ((), ()))):
        return lax.dot_general(x.astype(MXU_DTYPE), y.astype(MXU_DTYPE), dims, preferred_element_type=F32)

    def stack4(x):
        return jnp.where(same_head_rows, jnp.concatenate([x] * RW_QUAD, axis=0), 0.0)

    def chunk(c, carry):
        sl = pl.ds(pl.multiple_of(c * cs, cs), cs)
        lw = ops_scr[0, sl, :]
        cl = _split_dot_left(tri, lw)
        g_in = jnp.exp(cl)
        g_inv = jnp.exp(-cl)
        at_all = ops_scr[1, sl, :] * jnp.exp(cl - lw)
        bt_all = ops_scr[2, sl, :] * g_inv
        kt_all = ops_scr[3, sl, :] * g_inv
        rt_all = ops_scr[4, sl, :] * g_in
        v_all = ops_scr[5, sl, :]
        g_last = g_in[cs - 1:cs, :]
        quads = range(n_quads)
        qs = lambda x, q: x[:, q * qw:(q + 1) * qw]

        lhs = [jnp.concatenate([qs(at_all, q), qs(rt_all, q)], axis=0) for q in quads]
        vbd = [stack4(qs(v_all, q)) for q in quads]
        gram = [mm(lhs[q], jnp.concatenate([stack4(qs(bt_all, q)), stack4(qs(kt_all, q))], axis=0), _NT)
                for q in quads]
        l1 = [jnp.where(m_lbd, jnp.concatenate([gram[q][:cs, :nb]] * RW_QUAD, axis=0), 0.0)
              for q in quads]
        lak_v = [mm(jnp.where(m_ak, gram[q][:cs], 0.0), jnp.concatenate([vbd[q], vbd[q]], axis=0))
                 for q in quads]
        inv = [eye + l1[q] for q in quads]
        pw = l1
        for _ in range(lg_cs - 1):
            pw = [mm(pw[q], pw[q]) for q in quads]
            inv = [inv[q] + mm(inv[q], pw[q]) for q in quads]

        az = [mm(lhs[q], zt_scr[q], _NT) for q in quads]
        ubig = [mm(inv[q], stack4(az[q][:cs] + lak_v[q])) for q in quads]
        u = [sum(ubig[q][h * cs:(h + 1) * cs] for h in range(RW_QUAD)) for q in quads]
        y = [az[q][cs:] + mm(jnp.where(m_n, gram[q][cs:], 0.0),
                             jnp.concatenate([stack4(u[q]), vbd[q]], axis=0)) for q in quads]
        for q in quads:
            upd = mm(jnp.concatenate([u[q], qs(v_all, q)], axis=0),
                     jnp.concatenate([qs(bt_all, q), qs(kt_all, q)], axis=0), _TN)
            zt_scr[q] = (zt_scr[q] + jnp.where(m_state, upd, 0.0)) * qs(g_last, q)
            y_scr[sl, q * qw:(q + 1) * qw] = y[q]
        return carry

    lax.fori_loop(0, tb // cs, chunk, 0)

    y = y_scr[...]
    inv_hd = 1.0 / hd
    mean = head_sum(y) * inv_hd
    dlt = y - mean
    var = head_sum(dlt * dlt) * inv_hd
    yn = dlt * lax.rsqrt(var + RW_GN_EPS) * gng_ref[...] + gnb_ref[...]
    bonus = head_sum(r * k2 * rk_ref[...]) * v
    o_ref[...] = ((yn + bonus) * gt_ref[...]).astype(o_ref.dtype)


def _rw_chunk_scan(proj, w_pre, a_pre, g_out, w0, a0, k_k, k_a, r_k, gn_g, gn_b, bsz, seq, d):
    qw = RW_QUAD * RW_HEAD_DIM
    assert seq % RW_TBLK == 0 and RW_TBLK % RW_CHUNK == 0 and d % qw == 0
    assert 2 * RW_QUAD * RW_CHUNK == V7X_LANES
    n_quads = d // qw
    p3 = proj.reshape(bsz, seq, proj.shape[-1])
    act = lambda off: pl.BlockSpec((None, RW_TBLK, d), lambda b, t: (b, t, off))
    vec = pl.BlockSpec((1, d), lambda b, t: (0, 0))
    as3 = lambda z: z.reshape(bsz, seq, d)
    as_row = lambda z: z.reshape(1, d)
    return pl.pallas_call(
        _rw_chunk_kernel,
        out_shape=jax.ShapeDtypeStruct((bsz, seq, d), MXU_DTYPE),
        grid=(bsz, seq // RW_TBLK),
        in_specs=[act(0), act(1), act(2), act(0), act(0), act(0)] + [vec] * 7,
        out_specs=pl.BlockSpec((None, RW_TBLK, d), lambda b, t: (b, t, 0)),
        scratch_shapes=[
            pltpu.VMEM((n_quads, qw, qw), F32),
            pltpu.VMEM((6, RW_TBLK, d), F32),
            pltpu.VMEM((RW_TBLK, d), F32),
        ],
        compiler_params=_params(("parallel", "arbitrary")),
        name="rwkv_scan",
    )(p3, p3, p3, as3(w_pre), as3(a_pre), as3(g_out),
      as_row(w0), as_row(a0), as_row(k_k), as_row(k_a), as_row(r_k), as_row(gn_g), as_row(gn_b))


def _rw_layer(x2, bsz, seq, norm_g_mix, shift, scale, gate, mu, w_rkv, w0, w1, w2, a0, a1, a2, g1, g2,
              k_k, k_a, r_k, gn_g, gn_b, w_out):
    d = x2.shape[1]
    tn = 512
    pad_c = lambda m: jnp.pad(m, ((0, 0), (0, tn - m.shape[1])))
    pad_r = lambda m: jnp.pad(m, ((0, tn - m.shape[0]), (0, 0))).astype(MXU_DTYPE)
    w_all = jnp.concatenate([w_rkv[0], w_rkv[1], w_rkv[2], pad_c(w1), pad_c(a1), pad_c(g1)],
                            axis=1).astype(MXU_DTYPE)
    per = d // tn
    mix_ids = jnp.array([0] * per + [2] * per + [3] * per + [1, 4, 5], jnp.int32)
    proj = _rw_proj(x2, seq, norm_g_mix, shift, scale, mu, w_all, mix_ids, tn)
    w_pre = _lora_up(proj, 3 * per, pad_r(w2), "tanh", tn, "rwkv_decay_lora")
    a_pre = _lora_up(proj, 3 * per + 1, pad_r(a2), "none", tn, "rwkv_a_lora")
    g_out = _lora_up(proj, 3 * per + 2, pad_r(g2), "sigmoid", tn, "rwkv_gate_lora")
    yg = _rw_chunk_scan(proj, w_pre, a_pre, g_out, w0, a0, k_k, k_a, r_k, gn_g, gn_b, bsz, seq, d)
    return _res_matmul(yg.reshape(bsz * seq, d), w_out.astype(MXU_DTYPE), x2, seq, gate, "rwkv_out")


def kernel(x, c, norm_mix_g, norm_ffn_g, w_mod, b_mod, ffn_w_in, ffn_w_out, final_g, gla_w_in, gla_w_alpha_up, gla_b_alpha, gla_norm_g, gla_w_out, sb_w_in, sb_w_out, rw_mu, rw_w_rkv, rw_w0, rw_w1, rw_w2, rw_a0, rw_a1, rw_a2, rw_g1, rw_g2, rw_k_k, rw_k_a, rw_r_k, rw_gn_g, rw_gn_b, rw_w_out):
    bsz, seq, d = x.shape
    depth = w_mod.shape[0]
    mod = _mod_all(c, w_mod, b_mod)
    x2 = x.reshape(bsz * seq, d)
    for i in range(depth):
        sh_m, sc_m, gt_m, sh_f, sc_f, gt_f = (mod[i, :, n * d:(n + 1) * d] for n in range(6))
        kind, j = i % 3, i // 3
        if kind == 0:
            x2 = _gla_layer(x2, bsz, seq, norm_mix_g[i], sh_m, sc_m, gt_m, gla_w_in[j],
                            gla_w_alpha_up[j], gla_b_alpha[j], gla_norm_g[j], gla_w_out[j])
        elif kind == 1:
            x2 = _sb_layer(x2, bsz, seq, norm_mix_g[i], sh_m, sc_m, gt_m, sb_w_in[j], sb_w_out[j])
        else:
            x2 = _rw_layer(x2, bsz, seq, norm_mix_g[i], sh_m, sc_m, gt_m, rw_mu[j], rw_w_rkv[j],
                           rw_w0[j], rw_w1[j], rw_w2[j], rw_a0[j], rw_a1[j], rw_a2[j], rw_g1[j],
                           rw_g2[j], rw_k_k[j], rw_k_a[j], rw_r_k[j], rw_gn_g[j], rw_gn_b[j],
                           rw_w_out[j])
        x2 = _ffn(x2, seq, norm_ffn_g[i], sh_f, sc_f, gt_f,
                  ffn_w_in[i].astype(MXU_DTYPE), ffn_w_out[i].astype(MXU_DTYPE))
    return _final_norm(x2, final_g).reshape(bsz, seq, d)
```
